```python
import jax, jax.numpy as jnp
from jax import lax
import numpy as np

D_MODEL = 1024
BATCH = 8
SEQ = 4096
DEPTH = 1
DEC_BATCH = 32
DEC_SEQ = 8
PAST_LEN = 16384
PAGE_SIZE = 128

ATT_WIDTH = D_MODEL // 2
CONV_CH = D_MODEL - ATT_WIDTH
HEAD_DIM = 64
N_ATT_HEADS = ATT_WIDTH // HEAD_DIM
DILATED_BRANCHES = ((128, 1), (512, 4), (2048, 16))
W_MAX = 2048
Q_BLOCK = 128
CONV_K = 31
IN_COLS = 3 * ATT_WIDTH + 2 * CONV_CH
N_KEYS = 128
N_EXPERTS = N_KEYS * N_KEYS
PEER_HEADS = 8
PEER_KEY_DIM = 256
PEER_TOPK = 16
PEER_BLOCK = 256
ALPHA = (2.0 * DEPTH) ** 0.25
BETA = (8.0 * DEPTH) ** -0.25
LN_EPS = 1e-5

kernel_name = "hymba_conformer_dilated_peer_step"


def _layer_norm(x, g, b):
    xf = x.astype(jnp.float32)
    mu = jnp.mean(xf, -1, keepdims=True)
    var = jnp.mean(jnp.square(xf - mu), -1, keepdims=True)
    return ((xf - mu) * lax.rsqrt(var + LN_EPS) * g.astype(jnp.float32) + b.astype(jnp.float32)).astype(x.dtype)


def _masked_softmax_stats(s, mask):
    s = jnp.where(mask, s, -jnp.inf)
    m = jnp.max(s, -1, keepdims=True)
    p = jnp.exp(s - m)
    den = jnp.sum(p, -1, keepdims=True)
    return p / den, m[..., 0] + jnp.log(den[..., 0])


def _combine_branches(outs, lses, dtype):
    wts = jax.nn.softmax(jnp.stack(lses, 0), axis=0)
    o = outs[0] * wts[0][..., None]
    for i in range(1, len(outs)):
        o = o + outs[i] * wts[i][..., None]
    return o.astype(dtype)


def _in_proj(x, w_in):
    B, T, _ = x.shape
    z = jnp.einsum('btd,df->btf', x, w_in)
    q = z[..., 0:ATT_WIDTH].reshape(B, T, N_ATT_HEADS, HEAD_DIM)
    k = z[..., ATT_WIDTH:2 * ATT_WIDTH].reshape(B, T, N_ATT_HEADS, HEAD_DIM)
    v = z[..., 2 * ATT_WIDTH:3 * ATT_WIDTH].reshape(B, T, N_ATT_HEADS, HEAD_DIM)
    a = z[..., 3 * ATT_WIDTH:3 * ATT_WIDTH + CONV_CH]
    g = z[..., 3 * ATT_WIDTH + CONV_CH:]
    glu = a * jax.nn.sigmoid(g)
    return q, k, v, glu


def _dilated_attn_prompt(q, k, v):
    B, S, H, E = q.shape
    scale = HEAD_DIM ** -0.5
    pad = ((0, 0), (W_MAX, 0), (0, 0), (0, 0))
    kp = jnp.pad(k, pad)
    vp = jnp.pad(v, pad)

    def block(t0):
        qb = lax.dynamic_slice_in_dim(q, t0, Q_BLOCK, axis=1)
        outs, lses = [], []
        for w, d in DILATED_BRANCHES:
            L = w + Q_BLOCK
            kb = lax.dynamic_slice_in_dim(kp, t0 + W_MAX - w, L, axis=1)
            vb = lax.dynamic_slice_in_dim(vp, t0 + W_MAX - w, L, axis=1)
            qr = qb.reshape(B, Q_BLOCK // d, d, H, E)
            kr = kb.reshape(B, L // d, d, H, E)
            vr = vb.reshape(B, L // d, d, H, E).astype(jnp.float32)
            s = jnp.einsum('birhe,bjrhe->brhij', qr, kr,
                           preferred_element_type=jnp.float32) * scale
            i = jnp.arange(Q_BLOCK // d)[:, None]
            j = jnp.arange(L // d)[None, :]
            r = jnp.arange(d)[:, None, None]
            rel = j - i
            key_pos = t0 - w + j * d + r
            mask = (rel >= 0) & (rel <= w // d) & (key_pos >= 0)
            p, lse = _masked_softmax_stats(s, mask[:, None])
            o = jnp.einsum('brhij,bjrhe->birhe', p, vr)
            outs.append(o.reshape(B, Q_BLOCK, H, E))
            lses.append(lse.transpose(0, 3, 1, 2).reshape(B, Q_BLOCK, H))
        return _combine_branches(outs, lses, q.dtype)

    ob = lax.map(block, jnp.arange(S // Q_BLOCK) * Q_BLOCK)
    return ob.transpose(1, 0, 2, 3, 4).reshape(B, S, H, E)


def _dilated_attn_sample(q, k_all, v_all):
    B, T, H, E = q.shape
    Lb = k_all.shape[1] - T
    scale = HEAD_DIM ** -0.5
    outs, lses = [], []
    for w, d in DILATED_BRANCHES:
        n = w // d + 1
        idx = Lb + jnp.arange(T)[:, None] - d * jnp.arange(n)[None, :]
        valid = idx >= 0
        idxc = jnp.maximum(idx, 0)
        kg = jnp.take(k_all, idxc, axis=1)
        vg = jnp.take(v_all, idxc, axis=1).astype(jnp.float32)
        s = jnp.einsum('bthe,btnhe->bhtn', q, kg,
                       preferred_element_type=jnp.float32) * scale
        p, lse = _masked_softmax_stats(s, valid)
        outs.append(jnp.einsum('bhtn,btnhe->bthe', p, vg))
        lses.append(lse.transpose(0, 2, 1))
    return _combine_branches(outs, lses, q.dtype)


def _conv_module(glu, conv_past, conv_w, conv_b, ln_g, ln_b):
    xpad = jnp.concatenate([conv_past, glu], axis=1)
    y = lax.conv_general_dilated(xpad, conv_w[:, None, :].astype(xpad.dtype), window_strides=(1,),
                                 padding='VALID', dimension_numbers=('NWC', 'WIO', 'NWC'),
                                 feature_group_count=CONV_CH)
    y = _layer_norm(y + conv_b, ln_g, ln_b)
    return jax.nn.silu(y)


def _peer(x, w_query, sub_keys, expert_u, expert_v):
    B, T, D = x.shape
    xt = x.reshape(B * T, D)
    n = xt.shape[0]
    n_pad = (-n) % PEER_BLOCK
    blocks = jnp.pad(xt, ((0, n_pad), (0, 0))).reshape(-1, PEER_BLOCK, D)
    half = PEER_KEY_DIM // 2

    def block(xb):
        q = (xb @ w_query).reshape(PEER_BLOCK, PEER_HEADS, 2, half)
        s = jnp.einsum('thpc,hpkc->thpk', q, sub_keys, preferred_element_type=jnp.float32)
        top_s, top_i = lax.top_k(s, PEER_TOPK)
        cand_s = (top_s[:, :, 0, :, None] + top_s[:, :, 1, None, :]).reshape(PEER_BLOCK, PEER_HEADS, -1)
        cand_e = (top_i[:, :, 0, :, None] * N_KEYS + top_i[:, :, 1, None, :]).reshape(PEER_BLOCK, PEER_HEADS, -1)
        best_s, best_j = lax.top_k(cand_s, PEER_TOPK)
        experts = jnp.take_along_axis(cand_e, best_j, axis=-1)
        gate = jax.nn.softmax(best_s, axis=-1).reshape(PEER_BLOCK, -1)
        e_flat = experts.reshape(PEER_BLOCK, -1)
        u = jnp.take(expert_u, e_flat, axis=0)
        vv = jnp.take(expert_v, e_flat, axis=0)
        act = jax.nn.gelu(jnp.einsum('td,tkd->tk', xb, u, preferred_element_type=jnp.float32),
                          approximate=False)
        coef = (gate * act).astype(vv.dtype)
        return jnp.einsum('tk,tkd->td', coef, vv)

    out = lax.map(block, blocks).reshape(-1, D)[:n]
    return out.reshape(B, T, D).astype(x.dtype)


def _finish(x, att, conv, w_out, ln1_g, ln1_b, w_query, sub_keys, expert_u, expert_v, ln2_g, ln2_b):
    B, T, _ = x.shape
    mixed = jnp.concatenate([att.reshape(B, T, ATT_WIDTH), conv], axis=-1)
    mix = jnp.einsum('btf,fd->btd', mixed, w_out)
    x1 = _layer_norm(ALPHA * x + mix, ln1_g, ln1_b)
    return _layer_norm(ALPHA * x1 + _peer(x1, w_query, sub_keys, expert_u, expert_v), ln2_g, ln2_b)


def setup_inputs(seed: int = 0) -> dict:
    key = jax.random.key(seed)
    ks = jax.random.split(key, 20)
    f32 = jnp.float32
    wb = min(W_MAX, PAST_LEN)
    col_scale = jnp.concatenate([jnp.ones((2 * ATT_WIDTH,), f32),
                                 jnp.full((ATT_WIDTH + 2 * CONV_CH,), BETA, f32)])
    return {
        "x_prompt": jax.random.normal(ks[0], (BATCH, SEQ, D_MODEL), f32),
        "x_sample": jax.random.normal(ks[1], (DEC_BATCH, DEC_SEQ, D_MODEL), f32),
        "cache_k_win": jax.random.normal(ks[2], (DEPTH, DEC_BATCH, wb, N_ATT_HEADS, HEAD_DIM), f32),
        "cache_v_win": BETA * jax.random.normal(ks[3], (DEPTH, DEC_BATCH, wb, N_ATT_HEADS, HEAD_DIM), f32),
        "state_conv": 0.5 * jax.random.normal(ks[4], (DEPTH, DEC_BATCH, CONV_K - 1, CONV_CH), f32),
        "w_in": jax.random.normal(ks[5], (DEPTH, D_MODEL, IN_COLS), f32) * D_MODEL ** -0.5 * col_scale,
        "w_out": jax.random.normal(ks[6], (DEPTH, D_MODEL, D_MODEL), f32) * D_MODEL ** -0.5 * BETA,
        "conv_w": jax.random.normal(ks[7], (DEPTH, CONV_K, CONV_CH), f32) * CONV_K ** -0.5,
        "conv_b": 0.02 * jax.random.normal(ks[8], (DEPTH, CONV_CH), f32),
        "conv_ln_g": 1.0 + 0.02 * jax.random.normal(ks[9], (DEPTH, CONV_CH), f32),
        "conv_ln_b": 0.02 * jax.random.normal(ks[10], (DEPTH, CONV_CH), f32),
        "ln1_g": 1.0 + 0.02 * jax.random.normal(ks[11], (DEPTH, D_MODEL), f32),
        "ln1_b": 0.02 * jax.random.normal(ks[12], (DEPTH, D_MODEL), f32),
        "w_query": jax.random.normal(ks[13], (DEPTH, D_MODEL, PEER_HEADS * PEER_KEY_DIM), f32) * D_MODEL ** -0.5,
        "sub_keys": jax.random.normal(ks[14], (DEPTH, PEER_HEADS, 2, N_KEYS, PEER_KEY_DIM // 2), f32) * (PEER_KEY_DIM // 2) ** -0.5,
        "expert_u": jax.random.normal(ks[15], (DEPTH, N_EXPERTS, D_MODEL), f32) * D_MODEL ** -0.5 * BETA,
        "expert_v": jax.random.normal(ks[16], (DEPTH, N_EXPERTS, D_MODEL), f32) * 0.5 * BETA,
        "ln2_g": 1.0 + 0.02 * jax.random.normal(ks[17], (DEPTH, D_MODEL), f32),
        "ln2_b": 0.02 * jax.random.normal(ks[18], (DEPTH, D_MODEL), f32),
    }


def reference(x_prompt, x_sample, cache_k_win, cache_v_win, state_conv,
              w_in, w_out, conv_w, conv_b, conv_ln_g, conv_ln_b,
              ln1_g, ln1_b, w_query, sub_keys, expert_u, expert_v, ln2_g, ln2_b):
    hp, hs = x_prompt, x_sample
    kp_rows, vp_rows, cp_rows, ks_rows, vs_rows, cs_rows = [], [], [], [], [], []
    for l in range(DEPTH):
        S = hp.shape[1]
        q, k, v, glu = _in_proj(hp, w_in[l])
        att = _dilated_attn_prompt(q, k, v)
        conv = _conv_module(glu, jnp.zeros((hp.shape[0], CONV_K - 1, CONV_CH), glu.dtype),
                            conv_w[l], conv_b[l], conv_ln_g[l], conv_ln_b[l])
        keep_p = min(W_MAX, S)
        kp_rows.append(k[:, S - keep_p:])
        vp_rows.append(v[:, S - keep_p:])
        cp_rows.append(glu[:, S - (CONV_K - 1):])
        hp = _finish(hp, att, conv, w_out[l], ln1_g[l], ln1_b[l], w_query[l], sub_keys[l],
                     expert_u[l], expert_v[l], ln2_g[l], ln2_b[l])

        T = hs.shape[1]
        q2, k2, v2, glu2 = _in_proj(hs, w_in[l])
        k_all = jnp.concatenate([cache_k_win[l].astype(k2.dtype), k2], axis=1)
        v_all = jnp.concatenate([cache_v_win[l].astype(v2.dtype), v2], axis=1)
        att2 = _dilated_attn_sample(q2, k_all, v_all)
        conv_in = jnp.concatenate([state_conv[l].astype(glu2.dtype), glu2], axis=1)
        conv2 = _conv_module(glu2, state_conv[l].astype(glu2.dtype),
                             conv_w[l], conv_b[l], conv_ln_g[l], conv_ln_b[l])
        L_all = k_all.shape[1]
        keep_s = min(W_MAX, L_all)
        ks_rows.append(k_all[:, L_all - keep_s:])
        vs_rows.append(v_all[:, L_all - keep_s:])
        cs_rows.append(conv_in[:, conv_in.shape[1] - (CONV_K - 1):])
        hs = _finish(hs, att2, conv2, w_out[l], ln1_g[l], ln1_b[l], w_query[l], sub_keys[l],
                     expert_u[l], expert_v[l], ln2_g[l], ln2_b[l])

    new_k_prompt = jnp.stack(kp_rows, 0)
    new_v_prompt = jnp.stack(vp_rows, 0)
    new_conv_prompt = jnp.stack(cp_rows, 0)
    new_k_sample = jnp.stack(ks_rows, 0)
    new_v_sample = jnp.stack(vs_rows, 0)
    new_conv_sample = jnp.stack(cs_rows, 0)
    return (hp, hs, new_k_prompt, new_v_prompt, new_conv_prompt, new_k_sample, new_v_sample, new_conv_sample)
```

```python
import functools
import math

import jax
import jax.numpy as jnp
from jax import lax
from jax.experimental import pallas as pl
from jax.experimental.pallas import tpu as pltpu

F32 = jnp.float32
BF16 = jnp.bfloat16

HEAD_DIM = 64
N_KEYS = 128
PEER_HEADS = 8
PEER_TOPK = 16
SLOTS = PEER_HEADS * PEER_TOPK
CONV_K = 31
DILATIONS = (1, 4, 16)
WIN_STEPS = 128
LN_EPS = 1e-5

LANES = 128
ROW_TILE = 8
PACK_ROWS = 16
CHUNK = 16
VMEM_LIMIT = 56 * 1024 * 1024


def _cparams(sem):
    return pltpu.CompilerParams(dimension_semantics=sem, vmem_limit_bytes=VMEM_LIMIT)


def _nt_dot(a, b):
    return lax.dot_general(a, b, (((1,), (1,)), ((), ())), preferred_element_type=F32)


def _split_bf16(x):
    hi = x.astype(BF16)
    lo = (x - hi.astype(F32)).astype(BF16)
    return hi, lo


def _in_proj_kernel(att_w, x_ref, w_ref, q_ref, k_ref, v_ref, glu_ref):
    z = jnp.dot(x_ref[...].astype(BF16), w_ref[...], preferred_element_type=F32)
    q_ref[...] = z[:, :att_w]
    k_ref[...] = z[:, att_w:2 * att_w]
    v_ref[...] = z[:, 2 * att_w:3 * att_w]
    conv_ch = (z.shape[1] - 3 * att_w) // 2
    a = z[:, 3 * att_w:3 * att_w + conv_ch]
    g = z[:, 3 * att_w + conv_ch:]
    glu_ref[...] = a * (1.0 / (1.0 + jnp.exp(-g)))


def _in_proj(x, w_in, att_w, tm):
    n, d = x.shape
    cols = w_in.shape[1]
    conv_ch = (cols - 3 * att_w) // 2
    out = lambda w: jax.ShapeDtypeStruct((n, w), F32)
    ospec = lambda w: pl.BlockSpec((tm, w), lambda i: (i, 0))
    return pl.pallas_call(
        functools.partial(_in_proj_kernel, att_w),
        grid=(n // tm,),
        in_specs=[pl.BlockSpec((tm, d), lambda i: (i, 0)),
                  pl.BlockSpec((d, cols), lambda i: (0, 0))],
        out_specs=[ospec(att_w), ospec(att_w), ospec(att_w), ospec(conv_ch)],
        out_shape=[out(att_w), out(att_w), out(att_w), out(conv_ch)],
        compiler_params=_cparams(("arbitrary",)),
        name="in_proj",
    )(x, w_in)


Q_BLOCK = 128
COPY_ROWS = 256


def _attn_prompt_kernel(q_ref, k_ref, v_ref, o_ref, qs, ks, vs, on, mn, ln, ob, mb, lb):
    seq = q_ref.shape[0]
    scale = HEAD_DIM ** -0.5
    lane = lax.broadcasted_iota(jnp.int32, (1, LANES), 1)
    first_head = lane < HEAD_DIM
    qi = lax.broadcasted_iota(jnp.int32, (Q_BLOCK, 2 * Q_BLOCK), 0)
    kj = lax.broadcasted_iota(jnp.int32, (Q_BLOCK, 2 * Q_BLOCK), 1)
    rel = qi - kj

    for d in DILATIONS:
        n_sub = seq // d
        blocks_per_sub = n_sub // Q_BLOCK
        for r in range(d):
            for c in range(n_sub // COPY_ROWS):
                src = pl.ds(r + d * COPY_ROWS * c, COPY_ROWS, stride=d) if d > 1 else pl.ds(COPY_ROWS * c, COPY_ROWS)
                dst = pl.ds(r * n_sub + COPY_ROWS * c, COPY_ROWS)
                qs[dst, :] = (q_ref[src, :] * scale).astype(BF16)
                ks[dst, :] = k_ref[src, :].astype(BF16)
                vs[dst, :] = v_ref[src, :].astype(BF16)
        o_dst, m_dst, l_dst = (on, mn, ln) if d == 1 else (ob, mb, lb)

        def block(i, carry):
            il = i % blocks_per_sub
            has_prev = il > 0
            row0 = pl.multiple_of(i * Q_BLOCK, Q_BLOCK)
            kstart = pl.multiple_of(jnp.where(has_prev, row0 - Q_BLOCK, row0), Q_BLOCK)
            delta = rel + jnp.where(has_prev, Q_BLOCK, 0)
            valid = (delta >= 0) & (delta <= WIN_STEPS)
            qb = qs[pl.ds(row0, Q_BLOCK), :]
            kb = ks[pl.ds(kstart, 2 * Q_BLOCK), :]
            vb = vs[pl.ds(kstart, 2 * Q_BLOCK), :]
            outs = []
            for head_mask in (first_head, jnp.logical_not(first_head)):
                qh = jnp.where(head_mask, qb, jnp.zeros_like(qb))
                s = jnp.where(valid, _nt_dot(qh, kb), -jnp.inf)
                m = jnp.max(s, axis=1, keepdims=True)
                p = jnp.exp(s - m)
                l = jnp.sum(p, axis=1, keepdims=True)
                o = jnp.dot(p.astype(BF16), vb, preferred_element_type=F32)
                outs.append((o, m, l))
            (o0, m0, l0), (o1, m1, l1) = outs
            rows = pl.ds(row0, Q_BLOCK)
            o_dst[rows, :] = jnp.where(first_head, o0, o1)
            m_dst[rows, :] = jnp.where(first_head, m0, m1)
            l_dst[rows, :] = jnp.where(first_head, l0, l1)
            return carry

        lax.fori_loop(0, seq // Q_BLOCK, block, 0)

        if d > 1:
            for r in range(d):
                for c in range(n_sub // COPY_ROWS):
                    nat = pl.ds(r + d * COPY_ROWS * c, COPY_ROWS, stride=d)
                    sub = pl.ds(r * n_sub + COPY_ROWS * c, COPY_ROWS)
                    m1, m2 = mn[nat, :], mb[sub, :]
                    m = jnp.maximum(m1, m2)
                    a1, a2 = jnp.exp(m1 - m), jnp.exp(m2 - m)
                    on[nat, :] = on[nat, :] * a1 + ob[sub, :] * a2
                    ln[nat, :] = ln[nat, :] * a1 + lb[sub, :] * a2
                    mn[nat, :] = m

    for c in range(seq // COPY_ROWS):
        rows = pl.ds(COPY_ROWS * c, COPY_ROWS)
        o_ref[rows, :] = on[rows, :] / ln[rows, :]


def _attn_prompt(q, k, v, batch, seq):
    width = q.shape[1]
    spec = pl.BlockSpec((seq, LANES), lambda b, g: (b, g))
    f32_scr = pltpu.VMEM((seq, LANES), F32)
    bf_scr = pltpu.VMEM((seq, LANES), BF16)
    return pl.pallas_call(
        _attn_prompt_kernel,
        grid=(batch, width // LANES),
        in_specs=[spec, spec, spec],
        out_specs=spec,
        out_shape=jax.ShapeDtypeStruct(q.shape, F32),
        scratch_shapes=[bf_scr] * 3 + [f32_scr] * 6,
        compiler_params=_cparams(("arbitrary", "arbitrary")),
        name="attn_prompt",
    )(q, k, v)


KEY_PAD = 128


def _attn_sample_kernel(windows, q_ref, k_ref, v_ref, ck_ref, cv_ref, o_ref, nk_ref, nv_ref, kall, vall):
    t_new, width = q_ref.shape[1], q_ref.shape[2]
    lb = ck_ref.shape[1]
    heads = width // HEAD_DIM
    rows = heads * t_new
    scale = HEAD_DIM ** -0.5
    k_new, v_new = k_ref[0], v_ref[0]

    n_chunks = (lb - t_new) // COPY_ROWS
    for c in range(n_chunks):
        nk_ref[0, COPY_ROWS * c:COPY_ROWS * (c + 1), :] = ck_ref[0, t_new + COPY_ROWS * c:t_new + COPY_ROWS * (c + 1), :]
        nv_ref[0, COPY_ROWS * c:COPY_ROWS * (c + 1), :] = cv_ref[0, t_new + COPY_ROWS * c:t_new + COPY_ROWS * (c + 1), :]
    done = COPY_ROWS * n_chunks
    nk_ref[0, done:lb - t_new, :] = ck_ref[0, t_new + done:lb, :]
    nv_ref[0, done:lb - t_new, :] = cv_ref[0, t_new + done:lb, :]
    nk_ref[0, lb - t_new:lb, :] = k_new
    nv_ref[0, lb - t_new:lb, :] = v_new

    for c in range(lb // COPY_ROWS):
        sl = slice(COPY_ROWS * c, COPY_ROWS * (c + 1))
        kall[sl, :] = ck_ref[0, sl, :].astype(BF16)
        vall[sl, :] = cv_ref[0, sl, :].astype(BF16)
    pad = jnp.zeros((KEY_PAD - t_new, width), F32)
    kall[lb:lb + KEY_PAD, :] = jnp.concatenate([k_new, pad], axis=0).astype(BF16)
    vall[lb:lb + KEY_PAD, :] = jnp.concatenate([v_new, pad], axis=0).astype(BF16)

    q_rep = jnp.concatenate([q_ref[0] * scale] * heads, axis=0)
    r_head = lax.broadcasted_iota(jnp.int32, (rows, width), 0) // t_new
    l_head = lax.broadcasted_iota(jnp.int32, (rows, width), 1) // HEAD_DIM
    own = r_head == l_head
    qbd = jnp.where(own, q_rep, 0.0).astype(BF16)
    s = _nt_dot(qbd, kall[...])
    tok = lax.broadcasted_iota(jnp.int32, s.shape, 0) % t_new
    key = lax.broadcasted_iota(jnp.int32, s.shape, 1)
    dist = lb + tok - key
    cnt = jnp.zeros(s.shape, F32)
    for w, d in windows:
        cnt = cnt + ((dist >= 0) & (dist <= w) & (dist % d == 0)).astype(F32)
    s = jnp.where(cnt > 0, s, -jnp.inf)
    m = jnp.max(s, axis=1, keepdims=True)
    p = cnt * jnp.exp(s - m)
    l = jnp.sum(p, axis=1, keepdims=True)
    o = jnp.dot(p.astype(BF16), vall[...], preferred_element_type=F32) / l
    o = jnp.where(own, o, 0.0).reshape(heads, t_new, width)
    o_ref[0] = jnp.sum(o, axis=0)


def _attn_sample(q, k, v, cache_k, cache_v, windows):
    batch, t_new, width = q.shape
    lb = cache_k.shape[1]
    new_spec = pl.BlockSpec((1, t_new, width), lambda b: (b, 0, 0))
    cache_spec = pl.BlockSpec((1, lb, width), lambda b: (b, 0, 0))
    return pl.pallas_call(
        functools.partial(_attn_sample_kernel, windows),
        grid=(batch,),
        in_specs=[new_spec, new_spec, new_spec, cache_spec, cache_spec],
        out_specs=[new_spec, cache_spec, cache_spec],
        out_shape=[jax.ShapeDtypeStruct(q.shape, F32), jax.ShapeDtypeStruct(cache_k.shape, F32),
                   jax.ShapeDtypeStruct(cache_v.shape, F32)],
        scratch_shapes=[pltpu.VMEM((lb + KEY_PAD, width), BF16)] * 2,
        compiler_params=_cparams(("arbitrary",)),
        name="attn_sample",
    )(q, k, v, cache_k, cache_v)


HALO = 32


def _layer_norm_rows(h, gain, bias):
    mu = jnp.mean(h, axis=-1, keepdims=True)
    hc = h - mu
    var = jnp.mean(hc * hc, axis=-1, keepdims=True)
    return hc * lax.rsqrt(var + LN_EPS) * gain + bias


def _mix_kernel(alpha, chunk, x_ref, att_ref, glu_ref, halo_ref, past_ref, wo_ref, cw_ref, cb_ref, cg_ref,
                cbeta_ref, g1_ref, b1_ref, x1_ref, xp, conv_scr):
    tm, conv_ch = glu_ref.shape
    first_tile = pl.program_id(1) == 0
    xp[0:HALO, :] = jnp.where(first_tile, past_ref[0], halo_ref[...])
    xp[HALO:HALO + tm, :] = glu_ref[...]
    lead = HALO - (CONV_K - 1)
    for rc in range(tm // chunk):
        acc = jnp.zeros((chunk, conv_ch), F32)
        for tap in range(CONV_K):
            acc = acc + cw_ref[tap:tap + 1, :] * xp[rc * chunk + lead + tap:rc * chunk + lead + tap + chunk, :]
        y = _layer_norm_rows(acc + cb_ref[...], cg_ref[...], cbeta_ref[...])
        conv_scr[rc * chunk:(rc + 1) * chunk, :] = y * (1.0 / (1.0 + jnp.exp(-y)))
    att_w = att_ref.shape[1]
    mix = jnp.dot(att_ref[...].astype(BF16), wo_ref[0:att_w, :], preferred_element_type=F32)
    mix = mix + jnp.dot(conv_scr[...].astype(BF16), wo_ref[att_w:, :], preferred_element_type=F32)
    x1_ref[...] = _layer_norm_rows(alpha * x_ref[...] + mix, g1_ref[...], b1_ref[...])


def _mix(x, att, glu, past, w_out, conv_w, conv_b, conv_g, conv_beta, g1, b1, alpha, batch, seq, tm):
    n, d = x.shape
    att_w, conv_ch = att.shape[1], glu.shape[1]
    tiles = seq // tm
    past = jnp.pad(past, ((0, 0), (HALO - (CONV_K - 1), 0), (0, 0)))
    row = lambda w: pl.BlockSpec((tm, w), lambda b, i: (b * tiles + i, 0))
    vec = lambda w: pl.BlockSpec((1, w), lambda b, i: (0, 0))
    halo_blocks = n // HALO
    halo = pl.BlockSpec((HALO, conv_ch),
                        lambda b, i: (jnp.clip((b * seq + i * tm) // HALO - 1, 0, halo_blocks - 1), 0))
    return pl.pallas_call(
        functools.partial(_mix_kernel, alpha, min(tm, 64)),
        grid=(batch, tiles),
        in_specs=[row(d), row(att_w), row(conv_ch), halo,
                  pl.BlockSpec((1, HALO, conv_ch), lambda b, i: (b, 0, 0)),
                  pl.BlockSpec(w_out.shape, lambda b, i: (0, 0)),
                  pl.BlockSpec(conv_w.shape, lambda b, i: (0, 0)),
                  vec(conv_ch), vec(conv_ch), vec(conv_ch), vec(d), vec(d)],
        out_specs=row(d),
        out_shape=jax.ShapeDtypeStruct((n, d), F32),
        scratch_shapes=[pltpu.VMEM((HALO + tm, conv_ch), F32), pltpu.VMEM((tm, conv_ch), F32)],
        compiler_params=_cparams(("arbitrary", "arbitrary")),
        name="mix",
    )(x, att, glu, glu, past, w_out, conv_w, conv_b.reshape(1, -1), conv_g.reshape(1, -1),
      conv_beta.reshape(1, -1), g1.reshape(1, -1), b1.reshape(1, -1))


def _extract_max(s, iota):
    m = jnp.max(s, axis=0, keepdims=True)
    idx = jnp.min(jnp.where(s == m, iota, float(s.shape[0])), axis=0, keepdims=True)
    return m, idx, jnp.where(iota == idx, -jnp.inf, s)


def _route_kernel(half_experts, x_ref, wq_ref, sk_ref, m_ref, hi_ref, gate_ref, v0, i0, v1, i1, bs, be):
    tt = x_ref.shape[0]
    q = jnp.dot(x_ref[...].astype(BF16), wq_ref[...], preferred_element_type=F32)
    qb = q.astype(BF16)
    s0 = _nt_dot(sk_ref[0, 0], qb[:, :LANES])
    s1 = _nt_dot(sk_ref[0, 1], qb[:, LANES:])
    iota_k = lax.broadcasted_iota(jnp.int32, (N_KEYS, tt), 0).astype(F32)

    def first(r, carry):
        a, b = carry
        ma, ia, a = _extract_max(a, iota_k)
        mb, ib, b = _extract_max(b, iota_k)
        v0[pl.ds(r, 1), :] = ma
        i0[pl.ds(r, 1), :] = ia
        v1[pl.ds(r, 1), :] = mb
        i1[pl.ds(r, 1), :] = ib
        return a, b

    lax.fori_loop(0, PEER_TOPK, first, (s0, s1))

    cand_s = jnp.concatenate([v0[a:a + 1, :] + v1[...] for a in range(PEER_TOPK)], axis=0)
    cand_e = jnp.concatenate([i0[a:a + 1, :] * float(N_KEYS) + i1[...] for a in range(PEER_TOPK)], axis=0)
    iota_c = lax.broadcasted_iota(jnp.int32, cand_s.shape, 0).astype(F32)

    def second(r, c):
        m, idx, c = _extract_max(c, iota_c)
        bs[pl.ds(r, 1), :] = m
        be[pl.ds(r, 1), :] = jnp.max(jnp.where(iota_c == idx, cand_e, -1.0), axis=0, keepdims=True)
        return c

    lax.fori_loop(0, PEER_TOPK, second, cand_s)

    best = bs[...]
    p = jnp.exp(best - jnp.max(best, axis=0, keepdims=True))
    gate_ref[0] = p / jnp.sum(p, axis=0, keepdims=True)
    e = be[...].astype(jnp.int32)
    high = (e >= half_experts).astype(jnp.int32)
    hi_ref[0] = high
    m_ref[0] = e - high * half_experts


def _route(x, wq, sk, tt):
    n, d = x.shape
    half_experts = N_KEYS * N_KEYS // 2
    out = jax.ShapeDtypeStruct((PEER_HEADS, PEER_TOPK, n), jnp.int32)
    ospec = pl.BlockSpec((1, PEER_TOPK, tt), lambda i, h: (h, 0, i))
    return pl.pallas_call(
        functools.partial(_route_kernel, half_experts),
        grid=(n // tt, PEER_HEADS),
        in_specs=[
            pl.BlockSpec((tt, d), lambda i, h: (i, 0)),
            pl.BlockSpec((d, 2 * LANES), lambda i, h: (0, h)),
            pl.BlockSpec((1, 2, N_KEYS, LANES), lambda i, h: (h, 0, 0, 0)),
        ],
        out_specs=[ospec, ospec, ospec],
        out_shape=[out, out, jax.ShapeDtypeStruct(out.shape, F32)],
        scratch_shapes=[pltpu.VMEM((PEER_TOPK, tt), F32) for _ in range(6)],
        compiler_params=_cparams(("arbitrary", "arbitrary")),
        name="peer_route",
    )(x, wq, sk)


def _pack_table(tab):
    n, d = tab.shape
    half = n // 2
    t = tab.reshape(2, half, d // LANES, LANES).transpose(1, 0, 2, 3)
    return t.reshape(half, PACK_ROWS, LANES).astype(BF16)


def _diag_mask(rows):
    r = lax.broadcasted_iota(jnp.int32, (rows, CHUNK * PACK_ROWS), 0)
    c = lax.broadcasted_iota(jnp.int32, (rows, CHUNK * PACK_ROWS), 1)
    return (r % PACK_ROWS) == (c % PACK_ROWS)


def _peer_act_kernel(m_ref, x_ref, hi_ref, gate_ref, sel_ref, tab_ref, coef_ref, r_scr):
    tb = x_ref.shape[0]
    diag = _diag_mask(2 * PACK_ROWS)

    def token(t, carry):
        x = x_ref[t]
        xh, xl = _split_bf16(x)
        lhs = jnp.concatenate([xh, xh, xl, xl], axis=0)
        for c in range(SLOTS // CHUNK):
            tiles = [tab_ref[m_ref[t * SLOTS + c * CHUNK + i]] for i in range(CHUNK)]
            g = jnp.concatenate(tiles, axis=0)
            o = _nt_dot(lhs, g)
            o = jnp.where(diag, o, 0.0)
            r_scr[pl.ds(t, 1), c * 256:(c + 1) * 256] = jnp.sum(o, axis=0, keepdims=True)
        return carry

    lax.fori_loop(0, tb, token, 0)
    act2 = jnp.dot(r_scr[...], sel_ref[...], preferred_element_type=F32,
                   precision=lax.Precision.HIGHEST)
    act = jnp.where(hi_ref[...] > 0, act2[:, SLOTS:], act2[:, :SLOTS])
    gelu = 0.5 * act * (1.0 + lax.erf(act * (1.0 / math.sqrt(2.0))))
    coef_ref[...] = gate_ref[...] * gelu


def _peer_out_kernel(alpha, m_ref, x_ref, hi_ref, coef_ref, exp_ref, g_ref, b_ref, tab_ref, y_ref, ce_scr):
    tb = x_ref.shape[0]
    diag = _diag_mask(PACK_ROWS)
    coef = coef_ref[...]
    high = hi_ref[...] > 0
    c2 = jnp.concatenate([jnp.where(high, 0.0, coef), jnp.where(high, coef, 0.0)], axis=1)
    ce_scr[...] = jnp.dot(c2, exp_ref[...], preferred_element_type=F32,
                          precision=lax.Precision.HIGHEST)
    gain = g_ref[...]
    bias = b_ref[...]
    d_model = ROW_TILE * LANES

    def token(t, carry):
        acc = jnp.zeros((PACK_ROWS, LANES), F32)
        for c in range(SLOTS // CHUNK):
            tiles = [tab_ref[m_ref[t * SLOTS + c * CHUNK + i]] for i in range(CHUNK)]
            g = jnp.concatenate(tiles, axis=0)
            ce = ce_scr[pl.ds(t, 1), c * 256:(c + 1) * 256]
            cm = jnp.where(diag, jnp.broadcast_to(ce, (PACK_ROWS, 256)), 0.0)
            ch, cl = _split_bf16(cm)
            o = jnp.dot(jnp.concatenate([ch, cl], axis=0), g, preferred_element_type=F32)
            acc = acc + o[:PACK_ROWS] + o[PACK_ROWS:]
        h = alpha * x_ref[t] + acc[:ROW_TILE] + acc[ROW_TILE:]
        mu = jnp.sum(h) * (1.0 / d_model)
        hc = h - mu
        var = jnp.sum(hc * hc) * (1.0 / d_model)
        y_ref[t] = hc * lax.rsqrt(var + LN_EPS) * gain + bias
        return carry

    lax.fori_loop(0, tb, token, 0)


def _exp_matrix():
    return _sel_matrix().T


def _peer_out(m_flat, x3, hi, coef, tab, gain, bias, alpha, tb):
    n = x3.shape[0]
    expand = _exp_matrix()
    return pl.pallas_call(
        functools.partial(_peer_out_kernel, alpha),
        grid=(n // tb,),
        in_specs=[
            pl.BlockSpec((tb * SLOTS,), lambda i: (i,), memory_space=pltpu.SMEM),
            pl.BlockSpec((tb, ROW_TILE, LANES), lambda i: (i, 0, 0)),
            pl.BlockSpec((tb, SLOTS), lambda i: (i, 0)),
            pl.BlockSpec((tb, SLOTS), lambda i: (i, 0)),
            pl.BlockSpec(expand.shape, lambda i: (0, 0)),
            pl.BlockSpec((ROW_TILE, LANES), lambda i: (0, 0)),
            pl.BlockSpec((ROW_TILE, LANES), lambda i: (0, 0)),
            pl.BlockSpec(tab.shape, lambda i: (0, 0, 0), pipeline_mode=pl.Buffered(1)),
        ],
        out_specs=pl.BlockSpec((tb, ROW_TILE, LANES), lambda i: (i, 0, 0)),
        out_shape=jax.ShapeDtypeStruct((n, ROW_TILE, LANES), F32),
        scratch_shapes=[pltpu.VMEM((tb, SLOTS * PACK_ROWS), F32)],
        compiler_params=_cparams(("arbitrary",)),
        name="peer_out",
    )(m_flat, x3, hi, coef, expand, gain.reshape(ROW_TILE, LANES), bias.reshape(ROW_TILE, LANES), tab)


def _sel_matrix():
    k = jnp.arange(SLOTS * PACK_ROWS)
    slot, row = k // PACK_ROWS, k % PACK_ROWS
    col = jnp.where(row < ROW_TILE, slot, SLOTS + slot)
    return (col[:, None] == jnp.arange(2 * SLOTS)[None, :]).astype(F32)


def _peer_act(m_flat, x3, hi, gate, tab, tb):
    n = x3.shape[0]
    sel = _sel_matrix()
    return pl.pallas_call(
        _peer_act_kernel,
        grid=(n // tb,),
        in_specs=[
            pl.BlockSpec((tb * SLOTS,), lambda i: (i,), memory_space=pltpu.SMEM),
            pl.BlockSpec((tb, ROW_TILE, LANES), lambda i: (i, 0, 0)),
            pl.BlockSpec((tb, SLOTS), lambda i: (i, 0)),
            pl.BlockSpec((tb, SLOTS), lambda i: (i, 0)),
            pl.BlockSpec(sel.shape, lambda i: (0, 0)),
            pl.BlockSpec(tab.shape, lambda i: (0, 0, 0), pipeline_mode=pl.Buffered(1)),
        ],
        out_specs=pl.BlockSpec((tb, SLOTS), lambda i: (i, 0)),
        out_shape=jax.ShapeDtypeStruct((n, SLOTS), F32),
        scratch_shapes=[pltpu.VMEM((tb, SLOTS * PACK_ROWS), F32)],
        compiler_params=_cparams(("arbitrary",)),
        name="peer_act",
    )(m_flat, x3, hi, gate, sel, tab)


def _peer_layer(x1, wq, sk, u_tab, v_tab, gain, bias, alpha):
    n, d = x1.shape
    m, hi, gate = _route(x1, wq, sk, ROUTE_TILE)
    to_rows = lambda a: a.transpose(2, 0, 1).reshape(n, SLOTS)
    m_flat, hi, gate = to_rows(m).reshape(n * SLOTS), to_rows(hi), to_rows(gate)
    x3 = x1.reshape(n, ROW_TILE, LANES)
    coef = _peer_act(m_flat, x3, hi, gate, u_tab, PEER_TILE)
    y = _peer_out(m_flat, x3, hi, coef, v_tab, gain, bias, alpha, PEER_TILE)
    return y.reshape(n, d)


ROUTE_TILE = 256
PEER_TILE = 128
PROJ_TILE = 256
MIX_TILE = 256


def kernel(x_prompt, x_sample, cache_k_win, cache_v_win, state_conv, w_in, w_out, conv_w, conv_b, conv_ln_g,
           conv_ln_b, ln1_g, ln1_b, w_query, sub_keys, expert_u, expert_v, ln2_g, ln2_b):
    depth, d_model, in_cols = w_in.shape
    batch, seq, _ = x_prompt.shape
    dec_batch, dec_seq, _ = x_sample.shape
    conv_ch = conv_w.shape[2]
    att_w = (in_cols - 2 * conv_ch) // 3
    heads = att_w // HEAD_DIM
    lb = cache_k_win.shape[2]
    windows = tuple((WIN_STEPS * d, d) for d in DILATIONS)
    w_max = windows[-1][0]
    assert lb == w_max and seq >= w_max and d_model == ROW_TILE * LANES
    alpha = (2.0 * depth) ** 0.25

    hp = x_prompt.reshape(batch * seq, d_model)
    hs = x_sample.reshape(dec_batch * dec_seq, d_model)
    outs = [[] for _ in range(6)]
    for l in range(depth):
        w_in_l, w_out_l = w_in[l].astype(BF16), w_out[l].astype(BF16)
        wq_l, sk_l = w_query[l].astype(BF16), sub_keys[l].astype(BF16)
        u_tab, v_tab = _pack_table(expert_u[l]), _pack_table(expert_v[l])
        mix_args = (w_out_l, conv_w[l], conv_b[l], conv_ln_g[l], conv_ln_b[l], ln1_g[l], ln1_b[l], alpha)
        peer_args = (wq_l, sk_l, u_tab, v_tab, ln2_g[l], ln2_b[l], alpha)

        q, k, v, glu = _in_proj(hp, w_in_l, att_w, PROJ_TILE)
        att = _attn_prompt(q, k, v, batch, seq)
        past = jnp.zeros((batch, CONV_K - 1, conv_ch), F32)
        x1 = _mix(hp, att, glu, past, *mix_args, batch, seq, MIX_TILE)
        outs[0].append(k.reshape(batch, seq, heads, HEAD_DIM)[:, seq - w_max:])
        outs[1].append(v.reshape(batch, seq, heads, HEAD_DIM)[:, seq - w_max:])
        outs[2].append(glu.reshape(batch, seq, conv_ch)[:, seq - (CONV_K - 1):])
        hp = _peer_layer(x1, *peer_args)

        q, k, v, glu = _in_proj(hs, w_in_l, att_w, dec_batch * dec_seq)
        new3 = lambda a: a.reshape(dec_batch, dec_seq, att_w)
        att, nk, nv = _attn_sample(new3(q), new3(k), new3(v), cache_k_win[l].reshape(dec_batch, lb, att_w),
                                   cache_v_win[l].reshape(dec_batch, lb, att_w), windows)
        x1 = _mix(hs, att.reshape(dec_batch * dec_seq, att_w), glu, state_conv[l], *mix_args,
                  dec_batch, dec_seq, dec_seq)
        conv_in = jnp.concatenate([state_conv[l], glu.reshape(dec_batch, dec_seq, conv_ch)], axis=1)
        outs[3].append(nk.reshape(dec_batch, lb, heads, HEAD_DIM))
        outs[4].append(nv.reshape(dec_batch, lb, heads, HEAD_DIM))
        outs[5].append(conv_in[:, dec_seq:])
        hs = _peer_layer(x1, *peer_args)

    stack = lambda rows: jnp.stack(rows, 0)
    return (hp.reshape(batch, seq, d_model), hs.reshape(dec_batch, dec_seq, d_model),
            stack(outs[0]), stack(outs[1]), stack(outs[2]), stack(outs[3]), stack(outs[4]), stack(outs[5]))
```

```python
import functools
import math

import jax
import jax.numpy as jnp
from jax import lax
from jax.experimental import pallas as pl
from jax.experimental.pallas import tpu as pltpu

F32 = jnp.float32
BF16 = jnp.bfloat16

HEAD_DIM = 64
N_KEYS = 128
PEER_HEADS = 8
PEER_TOPK = 16
SLOTS = PEER_HEADS * PEER_TOPK
CONV_K = 31
DILATIONS = (1, 4, 16)
WIN_STEPS = 128
LN_EPS = 1e-5

LANES = 128
ROW_TILE = 8
PACK_ROWS = 16
CHUNK = 16
TOKEN_UNROLL = 8
VMEM_LIMIT = 56 * 1024 * 1024


def _cparams(sem):
    return pltpu.CompilerParams(dimension_semantics=sem, vmem_limit_bytes=VMEM_LIMIT)


def _nt_dot(a, b):
    return lax.dot_general(a, b, (((1,), (1,)), ((), ())), preferred_element_type=F32)


def _split_bf16(x):
    hi = x.astype(BF16)
    lo = (x - hi.astype(F32)).astype(BF16)
    return hi, lo


def _in_proj_kernel(att_w, x_ref, w_ref, q_ref, k_ref, v_ref, glu_ref):
    z = jnp.dot(x_ref[...].astype(BF16), w_ref[...], preferred_element_type=F32)
    q_ref[...] = z[:, :att_w]
    k_ref[...] = z[:, att_w:2 * att_w]
    v_ref[...] = z[:, 2 * att_w:3 * att_w]
    conv_ch = (z.shape[1] - 3 * att_w) // 2
    a = z[:, 3 * att_w:3 * att_w + conv_ch]
    g = z[:, 3 * att_w + conv_ch:]
    glu_ref[...] = a * (1.0 / (1.0 + jnp.exp(-g)))


def _in_proj(x, w_in, att_w, tm):
    n, d = x.shape
    cols = w_in.shape[1]
    conv_ch = (cols - 3 * att_w) // 2
    out = lambda w: jax.ShapeDtypeStruct((n, w), F32)
    ospec = lambda w: pl.BlockSpec((tm, w), lambda i: (i, 0))
    return pl.pallas_call(
        functools.partial(_in_proj_kernel, att_w),
        grid=(n // tm,),
        in_specs=[pl.BlockSpec((tm, d), lambda i: (i, 0)),
                  pl.BlockSpec((d, cols), lambda i: (0, 0))],
        out_specs=[ospec(att_w), ospec(att_w), ospec(att_w), ospec(conv_ch)],
        out_shape=[out(att_w), out(att_w), out(att_w), out(conv_ch)],
        compiler_params=_cparams(("arbitrary",)),
        name="in_proj",
    )(x, w_in)


Q_BLOCK = 128
ATTN_UNROLL = 4
COPY_ROWS = 256


def _attn_prompt_kernel(q_ref, k_ref, v_ref, o_ref, qs, ks, vs, on, mn, ln, ob, mb, lb):
    seq = q_ref.shape[0]
    scale = HEAD_DIM ** -0.5
    lane = lax.broadcasted_iota(jnp.int32, (1, LANES), 1)
    first_head = lane < HEAD_DIM
    qi = lax.broadcasted_iota(jnp.int32, (Q_BLOCK, 2 * Q_BLOCK), 0)
    kj = lax.broadcasted_iota(jnp.int32, (Q_BLOCK, 2 * Q_BLOCK), 1)
    rel = qi - kj

    for d in DILATIONS:
        n_sub = seq // d
        blocks_per_sub = n_sub // Q_BLOCK
        for r in range(d):
            for c in range(n_sub // COPY_ROWS):
                src = pl.ds(r + d * COPY_ROWS * c, COPY_ROWS, stride=d) if d > 1 else pl.ds(COPY_ROWS * c, COPY_ROWS)
                dst = pl.ds(r * n_sub + COPY_ROWS * c, COPY_ROWS)
                qs[dst, :] = (q_ref[src, :] * scale).astype(BF16)
                ks[dst, :] = k_ref[src, :].astype(BF16)
                vs[dst, :] = v_ref[src, :].astype(BF16)
        o_dst, m_dst, l_dst = (on, mn, ln) if d == 1 else (ob, mb, lb)

        def one_block(i):
            il = i % blocks_per_sub
            has_prev = il > 0
            row0 = pl.multiple_of(i * Q_BLOCK, Q_BLOCK)
            kstart = pl.multiple_of(jnp.where(has_prev, row0 - Q_BLOCK, row0), Q_BLOCK)
            delta = rel + jnp.where(has_prev, Q_BLOCK, 0)
            valid = (delta >= 0) & (delta <= WIN_STEPS)
            qb = qs[pl.ds(row0, Q_BLOCK), :]
            kb = ks[pl.ds(kstart, 2 * Q_BLOCK), :]
            vb = vs[pl.ds(kstart, 2 * Q_BLOCK), :]
            outs = []
            for head_mask in (first_head, jnp.logical_not(first_head)):
                qh = jnp.where(head_mask, qb, jnp.zeros_like(qb))
                s = jnp.where(valid, _nt_dot(qh, kb), -jnp.inf)
                m = jnp.max(s, axis=1, keepdims=True)
                p = jnp.exp(s - m)
                l = jnp.sum(p, axis=1, keepdims=True)
                o = jnp.dot(p.astype(BF16), vb, preferred_element_type=F32)
                outs.append((o, m, l))
            (o0, m0, l0), (o1, m1, l1) = outs
            rows = pl.ds(row0, Q_BLOCK)
            o_dst[rows, :] = jnp.where(first_head, o0, o1)
            m_dst[rows, :] = jnp.where(first_head, m0, m1)
            l_dst[rows, :] = jnp.where(first_head, l0, l1)

        def blocks(it, carry):
            for u in range(ATTN_UNROLL):
                one_block(it * ATTN_UNROLL + u)
            return carry

        lax.fori_loop(0, seq // (Q_BLOCK * ATTN_UNROLL), blocks, 0)

        if d > 1:
            for r in range(d):
                for c in range(n_sub // COPY_ROWS):
                    nat = pl.ds(r + d * COPY_ROWS * c, COPY_ROWS, stride=d)
                    sub = pl.ds(r * n_sub + COPY_ROWS * c, COPY_ROWS)
                    m1, m2 = mn[nat, :], mb[sub, :]
                    m = jnp.maximum(m1, m2)
                    a1, a2 = jnp.exp(m1 - m), jnp.exp(m2 - m)
                    on[nat, :] = on[nat, :] * a1 + ob[sub, :] * a2
                    ln[nat, :] = ln[nat, :] * a1 + lb[sub, :] * a2
                    mn[nat, :] = m

    for c in range(seq // COPY_ROWS):
        rows = pl.ds(COPY_ROWS * c, COPY_ROWS)
        o_ref[rows, :] = on[rows, :] / ln[rows, :]


def _attn_prompt(q, k, v, batch, seq):
    width = q.shape[1]
    spec = pl.BlockSpec((seq, LANES), lambda b, g: (b, g))
    f32_scr = pltpu.VMEM((seq, LANES), F32)
    bf_scr = pltpu.VMEM((seq, LANES), BF16)
    return pl.pallas_call(
        _attn_prompt_kernel,
        grid=(batch, width // LANES),
        in_specs=[spec, spec, spec],
        out_specs=spec,
        out_shape=jax.ShapeDtypeStruct(q.shape, F32),
        scratch_shapes=[bf_scr] * 3 + [f32_scr] * 6,
        compiler_params=_cparams(("arbitrary", "arbitrary")),
        name="attn_prompt",
    )(q, k, v)


KEY_PAD = 128


def _attn_sample_kernel(windows, q_ref, k_ref, v_ref, ck_ref, cv_ref, o_ref, nk_ref, nv_ref, kall, vall):
    t_new, width = q_ref.shape[1], q_ref.shape[2]
    lb = ck_ref.shape[1]
    heads = width // HEAD_DIM
    rows = heads * t_new
    scale = HEAD_DIM ** -0.5
    k_new, v_new = k_ref[0], v_ref[0]

    n_chunks = (lb - t_new) // COPY_ROWS
    for c in range(n_chunks):
        nk_ref[0, COPY_ROWS * c:COPY_ROWS * (c + 1), :] = ck_ref[0, t_new + COPY_ROWS * c:t_new + COPY_ROWS * (c + 1), :]
        nv_ref[0, COPY_ROWS * c:COPY_ROWS * (c + 1), :] = cv_ref[0, t_new + COPY_ROWS * c:t_new + COPY_ROWS * (c + 1), :]
    done = COPY_ROWS * n_chunks
    nk_ref[0, done:lb - t_new, :] = ck_ref[0, t_new + done:lb, :]
    nv_ref[0, done:lb - t_new, :] = cv_ref[0, t_new + done:lb, :]
    nk_ref[0, lb - t_new:lb, :] = k_new
    nv_ref[0, lb - t_new:lb, :] = v_new

    for c in range(lb // COPY_ROWS):
        sl = slice(COPY_ROWS * c, COPY_ROWS * (c + 1))
        kall[sl, :] = ck_ref[0, sl, :].astype(BF16)
        vall[sl, :] = cv_ref[0, sl, :].astype(BF16)
    pad = jnp.zeros((KEY_PAD - t_new, width), F32)
    kall[lb:lb + KEY_PAD, :] = jnp.concatenate([k_new, pad], axis=0).astype(BF16)
    vall[lb:lb + KEY_PAD, :] = jnp.concatenate([v_new, pad], axis=0).astype(BF16)

    q_rep = jnp.concatenate([q_ref[0] * scale] * heads, axis=0)
    r_head = lax.broadcasted_iota(jnp.int32, (rows, width), 0) // t_new
    l_head = lax.broadcasted_iota(jnp.int32, (rows, width), 1) // HEAD_DIM
    own = r_head == l_head
    qbd = jnp.where(own, q_rep, 0.0).astype(BF16)
    s = _nt_dot(qbd, kall[...])
    tok = lax.broadcasted_iota(jnp.int32, s.shape, 0) % t_new
    key = lax.broadcasted_iota(jnp.int32, s.shape, 1)
    dist = lb + tok - key
    cnt = jnp.zeros(s.shape, F32)
    for w, d in windows:
        cnt = cnt + ((dist >= 0) & (dist <= w) & (dist % d == 0)).astype(F32)
    s = jnp.where(cnt > 0, s, -jnp.inf)
    m = jnp.max(s, axis=1, keepdims=True)
    p = cnt * jnp.exp(s - m)
    l = jnp.sum(p, axis=1, keepdims=True)
    o = jnp.dot(p.astype(BF16), vall[...], preferred_element_type=F32) / l
    o = jnp.where(own, o, 0.0).reshape(heads, t_new, width)
    o_ref[0] = jnp.sum(o, axis=0)


def _attn_sample(q, k, v, cache_k, cache_v, windows):
    batch, t_new, width = q.shape
    lb = cache_k.shape[1]
    new_spec = pl.BlockSpec((1, t_new, width), lambda b: (b, 0, 0))
    cache_spec = pl.BlockSpec((1, lb, width), lambda b: (b, 0, 0))
    return pl.pallas_call(
        functools.partial(_attn_sample_kernel, windows),
        grid=(batch,),
        in_specs=[new_spec, new_spec, new_spec, cache_spec, cache_spec],
        out_specs=[new_spec, cache_spec, cache_spec],
        out_shape=[jax.ShapeDtypeStruct(q.shape, F32), jax.ShapeDtypeStruct(cache_k.shape, F32),
                   jax.ShapeDtypeStruct(cache_v.shape, F32)],
        scratch_shapes=[pltpu.VMEM((lb + KEY_PAD, width), BF16)] * 2,
        compiler_params=_cparams(("arbitrary",)),
        name="attn_sample",
    )(q, k, v, cache_k, cache_v)


HALO = 32


def _layer_norm_rows(h, gain, bias):
    mu = jnp.mean(h, axis=-1, keepdims=True)
    hc = h - mu
    var = jnp.mean(hc * hc, axis=-1, keepdims=True)
    return hc * lax.rsqrt(var + LN_EPS) * gain + bias


def _mix_kernel(alpha, chunk, x_ref, att_ref, glu_ref, halo_ref, past_ref, wo_ref, cw_ref, cb_ref, cg_ref,
                cbeta_ref, g1_ref, b1_ref, x1_ref, xp, conv_scr):
    tm, conv_ch = glu_ref.shape
    first_tile = pl.program_id(1) == 0
    xp[0:HALO, :] = jnp.where(first_tile, past_ref[0], halo_ref[...])
    xp[HALO:HALO + tm, :] = glu_ref[...]
    lead = HALO - (CONV_K - 1)
    for rc in range(tm // chunk):
        acc = jnp.zeros((chunk, conv_ch), F32)
        for tap in range(CONV_K):
            acc = acc + cw_ref[tap:tap + 1, :] * xp[rc * chunk + lead + tap:rc * chunk + lead + tap + chunk, :]
        y = _layer_norm_rows(acc + cb_ref[...], cg_ref[...], cbeta_ref[...])
        conv_scr[rc * chunk:(rc + 1) * chunk, :] = y * (1.0 / (1.0 + jnp.exp(-y)))
    att_w = att_ref.shape[1]
    mix = jnp.dot(att_ref[...].astype(BF16), wo_ref[0:att_w, :], preferred_element_type=F32)
    mix = mix + jnp.dot(conv_scr[...].astype(BF16), wo_ref[att_w:, :], preferred_element_type=F32)
    x1_ref[...] = _layer_norm_rows(alpha * x_ref[...] + mix, g1_ref[...], b1_ref[...])


def _mix(x, att, glu, past, w_out, conv_w, conv_b, conv_g, conv_beta, g1, b1, alpha, batch, seq, tm):
    n, d = x.shape
    att_w, conv_ch = att.shape[1], glu.shape[1]
    tiles = seq // tm
    past = jnp.pad(past, ((0, 0), (HALO - (CONV_K - 1), 0), (0, 0)))
    row = lambda w: pl.BlockSpec((tm, w), lambda b, i: (b * tiles + i, 0))
    vec = lambda w: pl.BlockSpec((1, w), lambda b, i: (0, 0))
    halo_blocks = n // HALO
    halo = pl.BlockSpec((HALO, conv_ch),
                        lambda b, i: (jnp.clip((b * seq + i * tm) // HALO - 1, 0, halo_blocks - 1), 0))
    return pl.pallas_call(
        functools.partial(_mix_kernel, alpha, min(tm, 64)),
        grid=(batch, tiles),
        in_specs=[row(d), row(att_w), row(conv_ch), halo,
                  pl.BlockSpec((1, HALO, conv_ch), lambda b, i: (b, 0, 0)),
                  pl.BlockSpec(w_out.shape, lambda b, i: (0, 0)),
                  pl.BlockSpec(conv_w.shape, lambda b, i: (0, 0)),
                  vec(conv_ch), vec(conv_ch), vec(conv_ch), vec(d), vec(d)],
        out_specs=row(d),
        out_shape=jax.ShapeDtypeStruct((n, d), F32),
        scratch_shapes=[pltpu.VMEM((HALO + tm, conv_ch), F32), pltpu.VMEM((tm, conv_ch), F32)],
        compiler_params=_cparams(("arbitrary", "arbitrary")),
        name="mix",
    )(x, att, glu, glu, past, w_out, conv_w, conv_b.reshape(1, -1), conv_g.reshape(1, -1),
      conv_beta.reshape(1, -1), g1.reshape(1, -1), b1.reshape(1, -1))


def _extract_max(s, iota):
    m = jnp.max(s, axis=0, keepdims=True)
    idx = jnp.min(jnp.where(s == m, iota, float(s.shape[0])), axis=0, keepdims=True)
    return m, idx, jnp.where(iota == idx, -jnp.inf, s)


def _route_kernel(half_experts, x_ref, wq_ref, sk_ref, m_ref, hi_ref, gate_ref, v0, i0, v1, i1, bs, be, cs, ce):
    tt = x_ref.shape[0]
    q = jnp.dot(x_ref[...].astype(BF16), wq_ref[...], preferred_element_type=F32)
    qb = q.astype(BF16)
    s0 = _nt_dot(sk_ref[0, 0], qb[:, :LANES])
    s1 = _nt_dot(sk_ref[0, 1], qb[:, LANES:])
    iota_k = lax.broadcasted_iota(jnp.int32, (N_KEYS, tt), 0).astype(F32)

    def first(r, carry):
        a, b = carry
        ma, ia, a = _extract_max(a, iota_k)
        mb, ib, b = _extract_max(b, iota_k)
        v0[pl.ds(r, 1), :] = ma
        i0[pl.ds(r, 1), :] = ia
        v1[pl.ds(r, 1), :] = mb
        i1[pl.ds(r, 1), :] = ib
        return a, b

    lax.fori_loop(0, PEER_TOPK, first, (s0, s1))

    off = 0
    for a in range(PEER_TOPK):
        nb = PEER_TOPK // (a + 1)
        cs[off:off + nb, :] = v0[a:a + 1, :] + v1[0:nb, :]
        ce[off:off + nb, :] = i0[a:a + 1, :] * float(N_KEYS) + i1[0:nb, :]
        off += nb
    cs[off:, :] = jnp.full((cs.shape[0] - off, tt), -jnp.inf, F32)
    ce[off:, :] = jnp.full((cs.shape[0] - off, tt), -1.0, F32)
    cand_s, cand_e = cs[...], ce[...]
    iota_c = lax.broadcasted_iota(jnp.int32, cand_s.shape, 0).astype(F32)

    def second(r, c):
        m, idx, c = _extract_max(c, iota_c)
        bs[pl.ds(r, 1), :] = m
        be[pl.ds(r, 1), :] = jnp.max(jnp.where(iota_c == idx, cand_e, -1.0), axis=0, keepdims=True)
        return c

    lax.fori_loop(0, PEER_TOPK, second, cand_s)

    best = bs[...]
    p = jnp.exp(best - jnp.max(best, axis=0, keepdims=True))
    gate_ref[0] = p / jnp.sum(p, axis=0, keepdims=True)
    e = be[...].astype(jnp.int32)
    high = (e >= half_experts).astype(jnp.int32)
    hi_ref[0] = high
    m_ref[0] = e - high * half_experts


def _route(x, wq, sk, tt):
    n, d = x.shape
    half_experts = N_KEYS * N_KEYS // 2
    cand_rows = -(-sum(PEER_TOPK // (a + 1) for a in range(PEER_TOPK)) // ROW_TILE) * ROW_TILE
    out = jax.ShapeDtypeStruct((PEER_HEADS, PEER_TOPK, n), jnp.int32)
    ospec = pl.BlockSpec((1, PEER_TOPK, tt), lambda i, h: (h, 0, i))
    return pl.pallas_call(
        functools.partial(_route_kernel, half_experts),
        grid=(n // tt, PEER_HEADS),
        in_specs=[
            pl.BlockSpec((tt, d), lambda i, h: (i, 0)),
            pl.BlockSpec((d, 2 * LANES), lambda i, h: (0, h)),
            pl.BlockSpec((1, 2, N_KEYS, LANES), lambda i, h: (h, 0, 0, 0)),
        ],
        out_specs=[ospec, ospec, ospec],
        out_shape=[out, out, jax.ShapeDtypeStruct(out.shape, F32)],
        scratch_shapes=[pltpu.VMEM((PEER_TOPK, tt), F32)] * 6 + [pltpu.VMEM((cand_rows, tt), F32)] * 2,
        compiler_params=_cparams(("arbitrary", "arbitrary")),
        name="peer_route",
    )(x, wq, sk)


def _pack_table(tab):
    n, d = tab.shape
    half = n // 2
    t = tab.reshape(2, half, d // LANES, LANES).transpose(1, 0, 2, 3)
    return t.reshape(half, PACK_ROWS, LANES).astype(BF16)


def _diag_mask(rows):
    r = lax.broadcasted_iota(jnp.int32, (rows, CHUNK * PACK_ROWS), 0)
    c = lax.broadcasted_iota(jnp.int32, (rows, CHUNK * PACK_ROWS), 1)
    return (r % PACK_ROWS) == (c % PACK_ROWS)


def _peer_act_kernel(m_ref, x_ref, hi_ref, gate_ref, sel_ref, tab_ref, coef_ref, r_scr):
    tb = x_ref.shape[0]
    diag = _diag_mask(2 * PACK_ROWS)

    def tokens(it, carry):
        for u in range(TOKEN_UNROLL):
            t = it * TOKEN_UNROLL + u
            xh, xl = _split_bf16(x_ref[t])
            lhs = jnp.concatenate([xh, xh, xl, xl], axis=0)
            for c in range(SLOTS // CHUNK):
                tiles = [tab_ref[m_ref[t * SLOTS + c * CHUNK + i]] for i in range(CHUNK)]
                g = jnp.concatenate(tiles, axis=0)
                o = jnp.where(diag, _nt_dot(lhs, g), 0.0)
                r_scr[pl.ds(t, 1), c * 256:(c + 1) * 256] = jnp.sum(o, axis=0, keepdims=True)
        return carry

    lax.fori_loop(0, tb // TOKEN_UNROLL, tokens, 0)
    act2 = jnp.dot(r_scr[...], sel_ref[...], preferred_element_type=F32,
                   precision=lax.Precision.HIGHEST)
    act = jnp.where(hi_ref[...] > 0, act2[:, SLOTS:], act2[:, :SLOTS])
    gelu = 0.5 * act * (1.0 + lax.erf(act * (1.0 / math.sqrt(2.0))))
    coef_ref[...] = gate_ref[...] * gelu


def _peer_out_kernel(alpha, m_ref, x_ref, hi_ref, coef_ref, exp_ref, g_ref, b_ref, tab_ref, y_ref, ce_scr):
    tb = x_ref.shape[0]
    diag = _diag_mask(PACK_ROWS)
    coef = coef_ref[...]
    high = hi_ref[...] > 0
    c2 = jnp.concatenate([jnp.where(high, 0.0, coef), jnp.where(high, coef, 0.0)], axis=1)
    ce_scr[...] = jnp.dot(c2, exp_ref[...], preferred_element_type=F32,
                          precision=lax.Precision.HIGHEST)
    def tokens(it, carry):
        for u in range(TOKEN_UNROLL):
            t = it * TOKEN_UNROLL + u
            acc = jnp.zeros((PACK_ROWS, LANES), F32)
            for c in range(SLOTS // CHUNK):
                tiles = [tab_ref[m_ref[t * SLOTS + c * CHUNK + i]] for i in range(CHUNK)]
                g = jnp.concatenate(tiles, axis=0)
                ce = ce_scr[pl.ds(t, 1), c * 256:(c + 1) * 256]
                cm = jnp.where(diag, jnp.broadcast_to(ce, (PACK_ROWS, 256)), 0.0)
                ch, cl = _split_bf16(cm)
                o = jnp.dot(jnp.concatenate([ch, cl], axis=0), g, preferred_element_type=F32)
                acc = acc + o[:PACK_ROWS] + o[PACK_ROWS:]
            y_ref[t] = alpha * x_ref[t] + acc[:ROW_TILE] + acc[ROW_TILE:]
        return carry

    lax.fori_loop(0, tb // TOKEN_UNROLL, tokens, 0)

    h = y_ref[...]
    inv_d = 1.0 / (ROW_TILE * LANES)
    total = lambda a: jnp.sum(jnp.sum(a, axis=2, keepdims=True), axis=1, keepdims=True)
    hc = h - total(h) * inv_d
    var = total(hc * hc) * inv_d
    y_ref[...] = hc * lax.rsqrt(var + LN_EPS) * g_ref[...] + b_ref[...]


def _exp_matrix():
    return _sel_matrix().T


def _peer_out(m_flat, x3, hi, coef, tab, gain, bias, alpha, tb):
    n = x3.shape[0]
    expand = _exp_matrix()
    return pl.pallas_call(
        functools.partial(_peer_out_kernel, alpha),
        grid=(n // tb,),
        in_specs=[
            pl.BlockSpec((tb * SLOTS,), lambda i: (i,), memory_space=pltpu.SMEM),
            pl.BlockSpec((tb, ROW_TILE, LANES), lambda i: (i, 0, 0)),
            pl.BlockSpec((tb, SLOTS), lambda i: (i, 0)),
            pl.BlockSpec((tb, SLOTS), lambda i: (i, 0)),
            pl.BlockSpec(expand.shape, lambda i: (0, 0)),
            pl.BlockSpec((ROW_TILE, LANES), lambda i: (0, 0)),
            pl.BlockSpec((ROW_TILE, LANES), lambda i: (0, 0)),
            pl.BlockSpec(tab.shape, lambda i: (0, 0, 0), pipeline_mode=pl.Buffered(1)),
        ],
        out_specs=pl.BlockSpec((tb, ROW_TILE, LANES), lambda i: (i, 0, 0)),
        out_shape=jax.ShapeDtypeStruct((n, ROW_TILE, LANES), F32),
        scratch_shapes=[pltpu.VMEM((tb, SLOTS * PACK_ROWS), F32)],
        compiler_params=_cparams(("arbitrary",)),
        name="peer_out",
    )(m_flat, x3, hi, coef, expand, gain.reshape(ROW_TILE, LANES), bias.reshape(ROW_TILE, LANES), tab)


def _sel_matrix():
    k = jnp.arange(SLOTS * PACK_ROWS)
    slot, row = k // PACK_ROWS, k % PACK_ROWS
    col = jnp.where(row < ROW_TILE, slot, SLOTS + slot)
    return (col[:, None] == jnp.arange(2 * SLOTS)[None, :]).astype(F32)


def _peer_act(m_flat, x3, hi, gate, tab, tb):
    n = x3.shape[0]
    sel = _sel_matrix()
    return pl.pallas_call(
        _peer_act_kernel,
        grid=(n // tb,),
        in_specs=[
            pl.BlockSpec((tb * SLOTS,), lambda i: (i,), memory_space=pltpu.SMEM),
            pl.BlockSpec((tb, ROW_TILE, LANES), lambda i: (i, 0, 0)),
            pl.BlockSpec((tb, SLOTS), lambda i: (i, 0)),
            pl.BlockSpec((tb, SLOTS), lambda i: (i, 0)),
            pl.BlockSpec(sel.shape, lambda i: (0, 0)),
            pl.BlockSpec(tab.shape, lambda i: (0, 0, 0), pipeline_mode=pl.Buffered(1)),
        ],
        out_specs=pl.BlockSpec((tb, SLOTS), lambda i: (i, 0)),
        out_shape=jax.ShapeDtypeStruct((n, SLOTS), F32),
        scratch_shapes=[pltpu.VMEM((tb, SLOTS * PACK_ROWS), F32)],
        compiler_params=_cparams(("arbitrary",)),
        name="peer_act",
    )(m_flat, x3, hi, gate, sel, tab)


def _peer_layer(x1, wq, sk, u_tab, v_tab, gain, bias, alpha):
    n, d = x1.shape
    m, hi, gate = _route(x1, wq, sk, ROUTE_TILE)
    to_rows = lambda a: a.transpose(2, 0, 1).reshape(n, SLOTS)
    m_flat, hi, gate = to_rows(m).reshape(n * SLOTS), to_rows(hi), to_rows(gate)
    x3 = x1.reshape(n, ROW_TILE, LANES)
    coef = _peer_act(m_flat, x3, hi, gate, u_tab, PEER_TILE)
    y = _peer_out(m_flat, x3, hi, coef, v_tab, gain, bias, alpha, PEER_TILE)
    return y.reshape(n, d)


ROUTE_TILE = 256
PEER_TILE = 128
PROJ_TILE = 256
MIX_TILE = 256


def kernel(x_prompt, x_sample, cache_k_win, cache_v_win, state_conv, w_in, w_out, conv_w, conv_b, conv_ln_g,
           conv_ln_b, ln1_g, ln1_b, w_query, sub_keys, expert_u, expert_v, ln2_g, ln2_b):
    depth, d_model, in_cols = w_in.shape
    batch, seq, _ = x_prompt.shape
    dec_batch, dec_seq, _ = x_sample.shape
    conv_ch = conv_w.shape[2]
    att_w = (in_cols - 2 * conv_ch) // 3
    heads = att_w // HEAD_DIM
    lb = cache_k_win.shape[2]
    windows = tuple((WIN_STEPS * d, d) for d in DILATIONS)
    w_max = windows[-1][0]
    assert lb == w_max and seq >= w_max and d_model == ROW_TILE * LANES
    alpha = (2.0 * depth) ** 0.25

    hp = x_prompt.reshape(batch * seq, d_model)
    hs = x_sample.reshape(dec_batch * dec_seq, d_model)
    outs = [[] for _ in range(6)]
    for l in range(depth):
        w_in_l, w_out_l = w_in[l].astype(BF16), w_out[l].astype(BF16)
        wq_l, sk_l = w_query[l].astype(BF16), sub_keys[l].astype(BF16)
        u_tab, v_tab = _pack_table(expert_u[l]), _pack_table(expert_v[l])
        mix_args = (w_out_l, conv_w[l], conv_b[l], conv_ln_g[l], conv_ln_b[l], ln1_g[l], ln1_b[l], alpha)
        peer_args = (wq_l, sk_l, u_tab, v_tab, ln2_g[l], ln2_b[l], alpha)

        q, k, v, glu = _in_proj(hp, w_in_l, att_w, PROJ_TILE)
        att = _attn_prompt(q, k, v, batch, seq)
        past = jnp.zeros((batch, CONV_K - 1, conv_ch), F32)
        x1 = _mix(hp, att, glu, past, *mix_args, batch, seq, MIX_TILE)
        outs[0].append(k.reshape(batch, seq, heads, HEAD_DIM)[:, seq - w_max:])
        outs[1].append(v.reshape(batch, seq, heads, HEAD_DIM)[:, seq - w_max:])
        outs[2].append(glu.reshape(batch, seq, conv_ch)[:, seq - (CONV_K - 1):])
        hp = _peer_layer(x1, *peer_args)

        q, k, v, glu = _in_proj(hs, w_in_l, att_w, dec_batch * dec_seq)
        new3 = lambda a: a.reshape(dec_batch, dec_seq, att_w)
        att, nk, nv = _attn_sample(new3(q), new3(k), new3(v), cache_k_win[l].reshape(dec_batch, lb, att_w),
                                   cache_v_win[l].reshape(dec_batch, lb, att_w), windows)
        x1 = _mix(hs, att.reshape(dec_batch * dec_seq, att_w), glu, state_conv[l], *mix_args,
                  dec_batch, dec_seq, dec_seq)
        conv_in = jnp.concatenate([state_conv[l], glu.reshape(dec_batch, dec_seq, conv_ch)], axis=1)
        outs[3].append(nk.reshape(dec_batch, lb, heads, HEAD_DIM))
        outs[4].append(nv.reshape(dec_batch, lb, heads, HEAD_DIM))
        outs[5].append(conv_in[:, dec_seq:])
        hs = _peer_layer(x1, *peer_args)

    stack = lambda rows: jnp.stack(rows, 0)
    return (hp.reshape(batch, seq, d_model), hs.reshape(dec_batch, dec_seq, d_model),
            stack(outs[0]), stack(outs[1]), stack(outs[2]), stack(outs[3]), stack(outs[4]), stack(outs[5]))
```

```python
import functools
import math

import jax
import jax.numpy as jnp
from jax import lax
from jax.experimental import pallas as pl
from jax.experimental.pallas import tpu as pltpu

F32 = jnp.float32
BF16 = jnp.bfloat16

HEAD_DIM = 64
N_KEYS = 128
PEER_HEADS = 8
PEER_TOPK = 16
SLOTS = PEER_HEADS * PEER_TOPK
CONV_K = 31
DILATIONS = (1, 4, 16)
WIN_STEPS = 128
LN_EPS = 1e-5

LANES = 128
ROW_TILE = 8
PACK_ROWS = 16
CHUNK = 16
TOKEN_UNROLL = 16
VMEM_LIMIT = 56 * 1024 * 1024


def _cparams(sem):
    return pltpu.CompilerParams(dimension_semantics=sem, vmem_limit_bytes=VMEM_LIMIT)


def _nt_dot(a, b):
    return lax.dot_general(a, b, (((1,), (1,)), ((), ())), preferred_element_type=F32)


def _split_bf16(x):
    hi = x.astype(BF16)
    lo = (x - hi.astype(F32)).astype(BF16)
    return hi, lo


def _in_proj_kernel(att_w, x_ref, w_ref, q_ref, k_ref, v_ref, glu_ref):
    z = jnp.dot(x_ref[...].astype(BF16), w_ref[...], preferred_element_type=F32)
    q_ref[...] = z[:, :att_w]
    k_ref[...] = z[:, att_w:2 * att_w]
    v_ref[...] = z[:, 2 * att_w:3 * att_w]
    conv_ch = (z.shape[1] - 3 * att_w) // 2
    a = z[:, 3 * att_w:3 * att_w + conv_ch]
    g = z[:, 3 * att_w + conv_ch:]
    glu_ref[...] = a * (1.0 / (1.0 + jnp.exp(-g)))


def _in_proj(x, w_in, att_w, tm):
    n, d = x.shape
    cols = w_in.shape[1]
    conv_ch = (cols - 3 * att_w) // 2
    out = lambda w: jax.ShapeDtypeStruct((n, w), F32)
    ospec = lambda w: pl.BlockSpec((tm, w), lambda i: (i, 0))
    return pl.pallas_call(
        functools.partial(_in_proj_kernel, att_w),
        grid=(n // tm,),
        in_specs=[pl.BlockSpec((tm, d), lambda i: (i, 0)),
                  pl.BlockSpec((d, cols), lambda i: (0, 0))],
        out_specs=[ospec(att_w), ospec(att_w), ospec(att_w), ospec(conv_ch)],
        out_shape=[out(att_w), out(att_w), out(att_w), out(conv_ch)],
        compiler_params=_cparams(("arbitrary",)),
        name="in_proj",
    )(x, w_in)


Q_BLOCK = 128
ATTN_UNROLL = 4
COPY_ROWS = 256


def _attn_prompt_kernel(q_ref, k_ref, v_ref, o_ref, qs, ks, vs, on, mn, ln, ob, mb, lb):
    seq = q_ref.shape[0]
    scale = HEAD_DIM ** -0.5
    lane = lax.broadcasted_iota(jnp.int32, (1, LANES), 1)
    first_head = lane < HEAD_DIM
    qi = lax.broadcasted_iota(jnp.int32, (Q_BLOCK, 2 * Q_BLOCK), 0)
    kj = lax.broadcasted_iota(jnp.int32, (Q_BLOCK, 2 * Q_BLOCK), 1)
    rel = qi - kj

    for d in DILATIONS:
        n_sub = seq // d
        blocks_per_sub = n_sub // Q_BLOCK
        for r in range(d):
            for c in range(n_sub // COPY_ROWS):
                src = pl.ds(r + d * COPY_ROWS * c, COPY_ROWS, stride=d) if d > 1 else pl.ds(COPY_ROWS * c, COPY_ROWS)
                dst = pl.ds(r * n_sub + COPY_ROWS * c, COPY_ROWS)
                qs[dst, :] = (q_ref[src, :] * scale).astype(BF16)
                ks[dst, :] = k_ref[src, :].astype(BF16)
                vs[dst, :] = v_ref[src, :].astype(BF16)
        o_dst, m_dst, l_dst = (on, mn, ln) if d == 1 else (ob, mb, lb)

        def one_block(i):
            il = i % blocks_per_sub
            has_prev = il > 0
            row0 = pl.multiple_of(i * Q_BLOCK, Q_BLOCK)
            kstart = pl.multiple_of(jnp.where(has_prev, row0 - Q_BLOCK, row0), Q_BLOCK)
            delta = rel + jnp.where(has_prev, Q_BLOCK, 0)
            valid = (delta >= 0) & (delta <= WIN_STEPS)
            qb = qs[pl.ds(row0, Q_BLOCK), :]
            kb = ks[pl.ds(kstart, 2 * Q_BLOCK), :]
            vb = vs[pl.ds(kstart, 2 * Q_BLOCK), :]
            outs = []
            for head_mask in (first_head, jnp.logical_not(first_head)):
                qh = jnp.where(head_mask, qb, jnp.zeros_like(qb))
                s = jnp.where(valid, _nt_dot(qh, kb), -jnp.inf)
                m = jnp.max(s, axis=1, keepdims=True)
                p = jnp.exp(s - m)
                l = jnp.sum(p, axis=1, keepdims=True)
                o = jnp.dot(p.astype(BF16), vb, preferred_element_type=F32)
                outs.append((o, m, l))
            (o0, m0, l0), (o1, m1, l1) = outs
            rows = pl.ds(row0, Q_BLOCK)
            o_dst[rows, :] = jnp.where(first_head, o0, o1)
            m_dst[rows, :] = jnp.where(first_head, m0, m1)
            l_dst[rows, :] = jnp.where(first_head, l0, l1)

        def blocks(it, carry):
            for u in range(ATTN_UNROLL):
                one_block(it * ATTN_UNROLL + u)
            return carry

        lax.fori_loop(0, seq // (Q_BLOCK * ATTN_UNROLL), blocks, 0)

        if d > 1:
            for r in range(d):
                for c in range(n_sub // COPY_ROWS):
                    nat = pl.ds(r + d * COPY_ROWS * c, COPY_ROWS, stride=d)
                    sub = pl.ds(r * n_sub + COPY_ROWS * c, COPY_ROWS)
                    m1, m2 = mn[nat, :], mb[sub, :]
                    m = jnp.maximum(m1, m2)
                    a1, a2 = jnp.exp(m1 - m), jnp.exp(m2 - m)
                    on[nat, :] = on[nat, :] * a1 + ob[sub, :] * a2
                    ln[nat, :] = ln[nat, :] * a1 + lb[sub, :] * a2
                    mn[nat, :] = m

    for c in range(seq // COPY_ROWS):
        rows = pl.ds(COPY_ROWS * c, COPY_ROWS)
        o_ref[rows, :] = on[rows, :] / ln[rows, :]


def _attn_prompt(q, k, v, batch, seq):
    width = q.shape[1]
    spec = pl.BlockSpec((seq, LANES), lambda b, g: (b, g))
    f32_scr = pltpu.VMEM((seq, LANES), F32)
    bf_scr = pltpu.VMEM((seq, LANES), BF16)
    return pl.pallas_call(
        _attn_prompt_kernel,
        grid=(batch, width // LANES),
        in_specs=[spec, spec, spec],
        out_specs=spec,
        out_shape=jax.ShapeDtypeStruct(q.shape, F32),
        scratch_shapes=[bf_scr] * 3 + [f32_scr] * 6,
        compiler_params=_cparams(("arbitrary", "arbitrary")),
        name="attn_prompt",
    )(q, k, v)


KEY_PAD = 128


def _attn_sample_kernel(windows, q_ref, k_ref, v_ref, ck_ref, cv_ref, o_ref, nk_ref, nv_ref, kall, vall):
    t_new, width = q_ref.shape[1], q_ref.shape[2]
    lb = ck_ref.shape[1]
    heads = width // HEAD_DIM
    rows = heads * t_new
    scale = HEAD_DIM ** -0.5
    k_new, v_new = k_ref[0], v_ref[0]

    n_chunks = (lb - t_new) // COPY_ROWS
    for c in range(n_chunks):
        nk_ref[0, COPY_ROWS * c:COPY_ROWS * (c + 1), :] = ck_ref[0, t_new + COPY_ROWS * c:t_new + COPY_ROWS * (c + 1), :]
        nv_ref[0, COPY_ROWS * c:COPY_ROWS * (c + 1), :] = cv_ref[0, t_new + COPY_ROWS * c:t_new + COPY_ROWS * (c + 1), :]
    done = COPY_ROWS * n_chunks
    nk_ref[0, done:lb - t_new, :] = ck_ref[0, t_new + done:lb, :]
    nv_ref[0, done:lb - t_new, :] = cv_ref[0, t_new + done:lb, :]
    nk_ref[0, lb - t_new:lb, :] = k_new
    nv_ref[0, lb - t_new:lb, :] = v_new

    for c in range(lb // COPY_ROWS):
        sl = slice(COPY_ROWS * c, COPY_ROWS * (c + 1))
        kall[sl, :] = ck_ref[0, sl, :].astype(BF16)
        vall[sl, :] = cv_ref[0, sl, :].astype(BF16)
    pad = jnp.zeros((KEY_PAD - t_new, width), F32)
    kall[lb:lb + KEY_PAD, :] = jnp.concatenate([k_new, pad], axis=0).astype(BF16)
    vall[lb:lb + KEY_PAD, :] = jnp.concatenate([v_new, pad], axis=0).astype(BF16)

    q_rep = jnp.concatenate([q_ref[0] * scale] * heads, axis=0)
    r_head = lax.broadcasted_iota(jnp.int32, (rows, width), 0) // t_new
    l_head = lax.broadcasted_iota(jnp.int32, (rows, width), 1) // HEAD_DIM
    own = r_head == l_head
    qbd = jnp.where(own, q_rep, 0.0).astype(BF16)
    s = _nt_dot(qbd, kall[...])
    tok = lax.broadcasted_iota(jnp.int32, s.shape, 0) % t_new
    key = lax.broadcasted_iota(jnp.int32, s.shape, 1)
    dist = lb + tok - key
    cnt = jnp.zeros(s.shape, F32)
    for w, d in windows:
        cnt = cnt + ((dist >= 0) & (dist <= w) & (dist % d == 0)).astype(F32)
    s = jnp.where(cnt > 0, s, -jnp.inf)
    m = jnp.max(s, axis=1, keepdims=True)
    p = cnt * jnp.exp(s - m)
    l = jnp.sum(p, axis=1, keepdims=True)
    o = jnp.dot(p.astype(BF16), vall[...], preferred_element_type=F32) / l
    o = jnp.where(own, o, 0.0).reshape(heads, t_new, width)
    o_ref[0] = jnp.sum(o, axis=0)


def _attn_sample(q, k, v, cache_k, cache_v, windows):
    batch, t_new, width = q.shape
    lb = cache_k.shape[1]
    new_spec = pl.BlockSpec((1, t_new, width), lambda b: (b, 0, 0))
    cache_spec = pl.BlockSpec((1, lb, width), lambda b: (b, 0, 0))
    return pl.pallas_call(
        functools.partial(_attn_sample_kernel, windows),
        grid=(batch,),
        in_specs=[new_spec, new_spec, new_spec, cache_spec, cache_spec],
        out_specs=[new_spec, cache_spec, cache_spec],
        out_shape=[jax.ShapeDtypeStruct(q.shape, F32), jax.ShapeDtypeStruct(cache_k.shape, F32),
                   jax.ShapeDtypeStruct(cache_v.shape, F32)],
        scratch_shapes=[pltpu.VMEM((lb + KEY_PAD, width), BF16)] * 2,
        compiler_params=_cparams(("arbitrary",)),
        name="attn_sample",
    )(q, k, v, cache_k, cache_v)


HALO = 32


def _layer_norm_rows(h, gain, bias):
    mu = jnp.mean(h, axis=-1, keepdims=True)
    hc = h - mu
    var = jnp.mean(hc * hc, axis=-1, keepdims=True)
    return hc * lax.rsqrt(var + LN_EPS) * gain + bias


def _mix_kernel(alpha, chunk, x_ref, att_ref, glu_ref, halo_ref, past_ref, wo_ref, cw_ref, cb_ref, cg_ref,
                cbeta_ref, g1_ref, b1_ref, x1_ref, xp, conv_scr):
    tm, conv_ch = glu_ref.shape
    first_tile = pl.program_id(1) == 0
    xp[0:HALO, :] = jnp.where(first_tile, past_ref[0], halo_ref[...])
    xp[HALO:HALO + tm, :] = glu_ref[...]
    lead = HALO - (CONV_K - 1)
    for rc in range(tm // chunk):
        acc = jnp.zeros((chunk, conv_ch), F32)
        for tap in range(CONV_K):
            acc = acc + cw_ref[tap:tap + 1, :] * xp[rc * chunk + lead + tap:rc * chunk + lead + tap + chunk, :]
        y = _layer_norm_rows(acc + cb_ref[...], cg_ref[...], cbeta_ref[...])
        conv_scr[rc * chunk:(rc + 1) * chunk, :] = y * (1.0 / (1.0 + jnp.exp(-y)))
    att_w = att_ref.shape[1]
    mix = jnp.dot(att_ref[...].astype(BF16), wo_ref[0:att_w, :], preferred_element_type=F32)
    mix = mix + jnp.dot(conv_scr[...].astype(BF16), wo_ref[att_w:, :], preferred_element_type=F32)
    x1_ref[...] = _layer_norm_rows(alpha * x_ref[...] + mix, g1_ref[...], b1_ref[...])


def _mix(x, att, glu, past, w_out, conv_w, conv_b, conv_g, conv_beta, g1, b1, alpha, batch, seq, tm):
    n, d = x.shape
    att_w, conv_ch = att.shape[1], glu.shape[1]
    tiles = seq // tm
    past = jnp.pad(past, ((0, 0), (HALO - (CONV_K - 1), 0), (0, 0)))
    row = lambda w: pl.BlockSpec((tm, w), lambda b, i: (b * tiles + i, 0))
    vec = lambda w: pl.BlockSpec((1, w), lambda b, i: (0, 0))
    halo_blocks = n // HALO
    halo = pl.BlockSpec((HALO, conv_ch),
                        lambda b, i: (jnp.clip((b * seq + i * tm) // HALO - 1, 0, halo_blocks - 1), 0))
    return pl.pallas_call(
        functools.partial(_mix_kernel, alpha, min(tm, 64)),
        grid=(batch, tiles),
        in_specs=[row(d), row(att_w), row(conv_ch), halo,
                  pl.BlockSpec((1, HALO, conv_ch), lambda b, i: (b, 0, 0)),
                  pl.BlockSpec(w_out.shape, lambda b, i: (0, 0)),
                  pl.BlockSpec(conv_w.shape, lambda b, i: (0, 0)),
                  vec(conv_ch), vec(conv_ch), vec(conv_ch), vec(d), vec(d)],
        out_specs=row(d),
        out_shape=jax.ShapeDtypeStruct((n, d), F32),
        scratch_shapes=[pltpu.VMEM((HALO + tm, conv_ch), F32), pltpu.VMEM((tm, conv_ch), F32)],
        compiler_params=_cparams(("arbitrary", "arbitrary")),
        name="mix",
    )(x, att, glu, glu, past, w_out, conv_w, conv_b.reshape(1, -1), conv_g.reshape(1, -1),
      conv_beta.reshape(1, -1), g1.reshape(1, -1), b1.reshape(1, -1))


def _extract_max(s, iota):
    m = jnp.max(s, axis=0, keepdims=True)
    idx = jnp.min(jnp.where(s == m, iota, float(s.shape[0])), axis=0, keepdims=True)
    return m, idx, jnp.where(iota == idx, -jnp.inf, s)


def _route_kernel(half_experts, x_ref, wq_ref, sk_ref, m_ref, hi_ref, gate_ref, v0, i0, v1, i1, bs, be, cs, ce):
    tt = x_ref.shape[0]
    q = jnp.dot(x_ref[...].astype(BF16), wq_ref[...], preferred_element_type=F32)
    qb = q.astype(BF16)
    s0 = _nt_dot(sk_ref[0, 0], qb[:, :LANES])
    s1 = _nt_dot(sk_ref[0, 1], qb[:, LANES:])
    iota_k = lax.broadcasted_iota(jnp.int32, (N_KEYS, tt), 0).astype(F32)

    def first(r, carry):
        a, b = carry
        ma, ia, a = _extract_max(a, iota_k)
        mb, ib, b = _extract_max(b, iota_k)
        v0[pl.ds(r, 1), :] = ma
        i0[pl.ds(r, 1), :] = ia
        v1[pl.ds(r, 1), :] = mb
        i1[pl.ds(r, 1), :] = ib
        return a, b

    lax.fori_loop(0, PEER_TOPK, first, (s0, s1))

    off = 0
    for a in range(PEER_TOPK):
        nb = PEER_TOPK // (a + 1)
        cs[off:off + nb, :] = v0[a:a + 1, :] + v1[0:nb, :]
        ce[off:off + nb, :] = i0[a:a + 1, :] * float(N_KEYS) + i1[0:nb, :]
        off += nb
    cs[off:, :] = jnp.full((cs.shape[0] - off, tt), -jnp.inf, F32)
    ce[off:, :] = jnp.full((cs.shape[0] - off, tt), -1.0, F32)
    cand_s, cand_e = cs[...], ce[...]
    iota_c = lax.broadcasted_iota(jnp.int32, cand_s.shape, 0).astype(F32)

    def second(r, c):
        m, idx, c = _extract_max(c, iota_c)
        bs[pl.ds(r, 1), :] = m
        be[pl.ds(r, 1), :] = jnp.max(jnp.where(iota_c == idx, cand_e, -1.0), axis=0, keepdims=True)
        return c

    lax.fori_loop(0, PEER_TOPK, second, cand_s)

    best = bs[...]
    p = jnp.exp(best - jnp.max(best, axis=0, keepdims=True))
    gate_ref[0] = p / jnp.sum(p, axis=0, keepdims=True)
    e = be[...].astype(jnp.int32)
    high = (e >= half_experts).astype(jnp.int32)
    hi_ref[0] = high
    m_ref[0] = (e - high * half_experts) * ROW_TILE


def _route(x, wq, sk, tt):
    n, d = x.shape
    half_experts = N_KEYS * N_KEYS // 2
    cand_rows = -(-sum(PEER_TOPK // (a + 1) for a in range(PEER_TOPK)) // ROW_TILE) * ROW_TILE
    out = jax.ShapeDtypeStruct((PEER_HEADS, PEER_TOPK, n), jnp.int32)
    ospec = pl.BlockSpec((1, PEER_TOPK, tt), lambda i, h: (h, 0, i))
    return pl.pallas_call(
        functools.partial(_route_kernel, half_experts),
        grid=(n // tt, PEER_HEADS),
        in_specs=[
            pl.BlockSpec((tt, d), lambda i, h: (i, 0)),
            pl.BlockSpec((d, 2 * LANES), lambda i, h: (0, h)),
            pl.BlockSpec((1, 2, N_KEYS, LANES), lambda i, h: (h, 0, 0, 0)),
        ],
        out_specs=[ospec, ospec, ospec],
        out_shape=[out, out, jax.ShapeDtypeStruct(out.shape, F32)],
        scratch_shapes=[pltpu.VMEM((PEER_TOPK, tt), F32)] * 6 + [pltpu.VMEM((cand_rows, tt), F32)] * 2,
        compiler_params=_cparams(("arbitrary", "arbitrary")),
        name="peer_route",
    )(x, wq, sk)


def _pack_table(tab):
    n, d = tab.shape
    half = n // 2
    t = tab.reshape(2, half, d // LANES, LANES).transpose(1, 0, 2, 3)
    t = t.reshape(half, ROW_TILE, 2, LANES).astype(BF16).transpose(0, 1, 3, 2)
    return lax.bitcast_convert_type(t, jnp.int32).reshape(half * ROW_TILE, LANES)


def _load_tile(tab_ref, row):
    return pltpu.bitcast(tab_ref[pl.ds(pl.multiple_of(row, ROW_TILE), ROW_TILE), :], BF16)


def _diag_mask(rows):
    r = lax.broadcasted_iota(jnp.int32, (rows, CHUNK * PACK_ROWS), 0)
    c = lax.broadcasted_iota(jnp.int32, (rows, CHUNK * PACK_ROWS), 1)
    return (r % PACK_ROWS) == (c % PACK_ROWS)


def _peer_act_kernel(m_ref, x_ref, hi_ref, gate_ref, sel_ref, tab_ref, coef_ref, r_scr):
    tb = x_ref.shape[0]
    diag = _diag_mask(2 * PACK_ROWS)

    def tokens(it, carry):
        for u in range(TOKEN_UNROLL):
            t = it * TOKEN_UNROLL + u
            xh, xl = _split_bf16(x_ref[t])
            lhs = jnp.concatenate([xh, xh, xl, xl], axis=0)
            for c in range(SLOTS // CHUNK):
                tiles = [_load_tile(tab_ref, m_ref[t * SLOTS + c * CHUNK + i]) for i in range(CHUNK)]
                g = jnp.concatenate(tiles, axis=0)
                o = jnp.where(diag, _nt_dot(lhs, g), 0.0)
                r_scr[pl.ds(t, 1), c * 256:(c + 1) * 256] = jnp.sum(o, axis=0, keepdims=True)
        return carry

    lax.fori_loop(0, tb // TOKEN_UNROLL, tokens, 0)
    act2 = jnp.dot(r_scr[...], sel_ref[...], preferred_element_type=F32,
                   precision=lax.Precision.HIGHEST)
    act = jnp.where(hi_ref[...] > 0, act2[:, SLOTS:], act2[:, :SLOTS])
    gelu = 0.5 * act * (1.0 + lax.erf(act * (1.0 / math.sqrt(2.0))))
    coef_ref[...] = gate_ref[...] * gelu


def _peer_out_kernel(alpha, m_ref, x_ref, hi_ref, coef_ref, exp_ref, g_ref, b_ref, tab_ref, y_ref, ce_scr):
    tb = x_ref.shape[0]
    diag = _diag_mask(PACK_ROWS)
    coef = coef_ref[...]
    high = hi_ref[...] > 0
    c2 = jnp.concatenate([jnp.where(high, 0.0, coef), jnp.where(high, coef, 0.0)], axis=1)
    ce_scr[...] = jnp.dot(c2, exp_ref[...], preferred_element_type=F32,
                          precision=lax.Precision.HIGHEST)
    def tokens(it, carry):
        for u in range(TOKEN_UNROLL):
            t = it * TOKEN_UNROLL + u
            acc = jnp.zeros((PACK_ROWS, LANES), F32)
            for c in range(SLOTS // CHUNK):
                tiles = [_load_tile(tab_ref, m_ref[t * SLOTS + c * CHUNK + i]) for i in range(CHUNK)]
                g = jnp.concatenate(tiles, axis=0)
                ce = ce_scr[pl.ds(t, 1), c * 256:(c + 1) * 256]
                cm = jnp.where(diag, jnp.broadcast_to(ce, (PACK_ROWS, 256)), 0.0)
                ch, cl = _split_bf16(cm)
                o = jnp.dot(jnp.concatenate([ch, cl], axis=0), g, preferred_element_type=F32)
                acc = acc + o[:PACK_ROWS] + o[PACK_ROWS:]
            y_ref[t] = alpha * x_ref[t] + acc[:ROW_TILE] + acc[ROW_TILE:]
        return carry

    lax.fori_loop(0, tb // TOKEN_UNROLL, tokens, 0)

    h = y_ref[...]
    inv_d = 1.0 / (ROW_TILE * LANES)
    total = lambda a: jnp.sum(jnp.sum(a, axis=2, keepdims=True), axis=1, keepdims=True)
    hc = h - total(h) * inv_d
    var = total(hc * hc) * inv_d
    y_ref[...] = hc * lax.rsqrt(var + LN_EPS) * g_ref[...] + b_ref[...]


def _exp_matrix():
    return _sel_matrix().T


def _peer_out(m_flat, x3, hi, coef, tab, gain, bias, alpha, tb):
    n = x3.shape[0]
    expand = _exp_matrix()
    return pl.pallas_call(
        functools.partial(_peer_out_kernel, alpha),
        grid=(n // tb,),
        in_specs=[
            pl.BlockSpec((tb * SLOTS,), lambda i: (i,), memory_space=pltpu.SMEM),
            pl.BlockSpec((tb, ROW_TILE, LANES), lambda i: (i, 0, 0)),
            pl.BlockSpec((tb, SLOTS), lambda i: (i, 0)),
            pl.BlockSpec((tb, SLOTS), lambda i: (i, 0)),
            pl.BlockSpec(expand.shape, lambda i: (0, 0)),
            pl.BlockSpec((ROW_TILE, LANES), lambda i: (0, 0)),
            pl.BlockSpec((ROW_TILE, LANES), lambda i: (0, 0)),
            pl.BlockSpec(tab.shape, lambda i: (0, 0), pipeline_mode=pl.Buffered(1)),
        ],
        out_specs=pl.BlockSpec((tb, ROW_TILE, LANES), lambda i: (i, 0, 0)),
        out_shape=jax.ShapeDtypeStruct((n, ROW_TILE, LANES), F32),
        scratch_shapes=[pltpu.VMEM((tb, SLOTS * PACK_ROWS), F32)],
        compiler_params=_cparams(("arbitrary",)),
        name="peer_out",
    )(m_flat, x3, hi, coef, expand, gain.reshape(ROW_TILE, LANES), bias.reshape(ROW_TILE, LANES), tab)


def _sel_matrix():
    k = jnp.arange(SLOTS * PACK_ROWS)
    slot, row = k // PACK_ROWS, k % PACK_ROWS
    col = jnp.where(row < ROW_TILE, slot, SLOTS + slot)
    return (col[:, None] == jnp.arange(2 * SLOTS)[None, :]).astype(F32)


def _peer_act(m_flat, x3, hi, gate, tab, tb):
    n = x3.shape[0]
    sel = _sel_matrix()
    return pl.pallas_call(
        _peer_act_kernel,
        grid=(n // tb,),
        in_specs=[
            pl.BlockSpec((tb * SLOTS,), lambda i: (i,), memory_space=pltpu.SMEM),
            pl.BlockSpec((tb, ROW_TILE, LANES), lambda i: (i, 0, 0)),
            pl.BlockSpec((tb, SLOTS), lambda i: (i, 0)),
            pl.BlockSpec((tb, SLOTS), lambda i: (i, 0)),
            pl.BlockSpec(sel.shape, lambda i: (0, 0)),
            pl.BlockSpec(tab.shape, lambda i: (0, 0), pipeline_mode=pl.Buffered(1)),
        ],
        out_specs=pl.BlockSpec((tb, SLOTS), lambda i: (i, 0)),
        out_shape=jax.ShapeDtypeStruct((n, SLOTS), F32),
        scratch_shapes=[pltpu.VMEM((tb, SLOTS * PACK_ROWS), F32)],
        compiler_params=_cparams(("arbitrary",)),
        name="peer_act",
    )(m_flat, x3, hi, gate, sel, tab)


def _peer_layer(x1, wq, sk, u_tab, v_tab, gain, bias, alpha):
    n, d = x1.shape
    m, hi, gate = _route(x1, wq, sk, min(ROUTE_TILE, n))
    to_rows = lambda a: a.transpose(2, 0, 1).reshape(n, SLOTS)
    m_flat, hi, gate = to_rows(m).reshape(n * SLOTS), to_rows(hi), to_rows(gate)
    x3 = x1.reshape(n, ROW_TILE, LANES)
    coef = _peer_act(m_flat, x3, hi, gate, u_tab, PEER_TILE)
    y = _peer_out(m_flat, x3, hi, coef, v_tab, gain, bias, alpha, PEER_TILE)
    return y.reshape(n, d)


ROUTE_TILE = 512
PEER_TILE = 128
PROJ_TILE = 256
MIX_TILE = 256


def kernel(x_prompt, x_sample, cache_k_win, cache_v_win, state_conv, w_in, w_out, conv_w, conv_b, conv_ln_g,
           conv_ln_b, ln1_g, ln1_b, w_query, sub_keys, expert_u, expert_v, ln2_g, ln2_b):
    depth, d_model, in_cols = w_in.shape
    batch, seq, _ = x_prompt.shape
    dec_batch, dec_seq, _ = x_sample.shape
    conv_ch = conv_w.shape[2]
    att_w = (in_cols - 2 * conv_ch) // 3
    heads = att_w // HEAD_DIM
    lb = cache_k_win.shape[2]
    windows = tuple((WIN_STEPS * d, d) for d in DILATIONS)
    w_max = windows[-1][0]
    assert lb == w_max and seq >= w_max and d_model == ROW_TILE * LANES
    alpha = (2.0 * depth) ** 0.25

    hp = x_prompt.reshape(batch * seq, d_model)
    hs = x_sample.reshape(dec_batch * dec_seq, d_model)
    outs = [[] for _ in range(6)]
    for l in range(depth):
        w_in_l, w_out_l = w_in[l].astype(BF16), w_out[l].astype(BF16)
        wq_l, sk_l = w_query[l].astype(BF16), sub_keys[l].astype(BF16)
        u_tab, v_tab = _pack_table(expert_u[l]), _pack_table(expert_v[l])
        mix_args = (w_out_l, conv_w[l], conv_b[l], conv_ln_g[l], conv_ln_b[l], ln1_g[l], ln1_b[l], alpha)
        peer_args = (wq_l, sk_l, u_tab, v_tab, ln2_g[l], ln2_b[l], alpha)

        q, k, v, glu = _in_proj(hp, w_in_l, att_w, PROJ_TILE)
        att = _attn_prompt(q, k, v, batch, seq)
        past = jnp.zeros((batch, CONV_K - 1, conv_ch), F32)
        x1 = _mix(hp, att, glu, past, *mix_args, batch, seq, MIX_TILE)
        outs[0].append(k.reshape(batch, seq, heads, HEAD_DIM)[:, seq - w_max:])
        outs[1].append(v.reshape(batch, seq, heads, HEAD_DIM)[:, seq - w_max:])
        outs[2].append(glu.reshape(batch, seq, conv_ch)[:, seq - (CONV_K - 1):])
        hp = _peer_layer(x1, *peer_args)

        q, k, v, glu = _in_proj(hs, w_in_l, att_w, dec_batch * dec_seq)
        new3 = lambda a: a.reshape(dec_batch, dec_seq, att_w)
        att, nk, nv = _attn_sample(new3(q), new3(k), new3(v), cache_k_win[l].reshape(dec_batch, lb, att_w),
                                   cache_v_win[l].reshape(dec_batch, lb, att_w), windows)
        x1 = _mix(hs, att.reshape(dec_batch * dec_seq, att_w), glu, state_conv[l], *mix_args,
                  dec_batch, dec_seq, dec_seq)
        conv_in = jnp.concatenate([state_conv[l], glu.reshape(dec_batch, dec_seq, conv_ch)], axis=1)
        outs[3].append(nk.reshape(dec_batch, lb, heads, HEAD_DIM))
        outs[4].append(nv.reshape(dec_batch, lb, heads, HEAD_DIM))
        outs[5].append(conv_in[:, dec_seq:])
        hs = _peer_layer(x1, *peer_args)

    stack = lambda rows: jnp.stack(rows, 0)
    return (hp.reshape(batch, seq, d_model), hs.reshape(dec_batch, dec_seq, d_model),
            stack(outs[0]), stack(outs[1]), stack(outs[2]), stack(outs[3]), stack(outs[4]), stack(outs[5]))
```

```python
import functools
import math

import jax
import jax.numpy as jnp
from jax import lax
from jax.experimental import pallas as pl
from jax.experimental.pallas import tpu as pltpu

F32 = jnp.float32
BF16 = jnp.bfloat16

HEAD_DIM = 64
N_KEYS = 128
PEER_HEADS = 8
PEER_TOPK = 16
SLOTS = PEER_HEADS * PEER_TOPK
CONV_K = 31
DILATIONS = (1, 4, 16)
WIN_STEPS = 128
LN_EPS = 1e-5

LANES = 128
ROW_TILE = 8
PACK_ROWS = 16
CHUNK = 16
TOKEN_UNROLL = 16
ACT_LHS_COPIES = 4
OUT_LHS_COPIES = 3
VMEM_LIMIT = 56 * 1024 * 1024


def _cparams(sem):
    return pltpu.CompilerParams(dimension_semantics=sem, vmem_limit_bytes=VMEM_LIMIT)


def _nt_dot(a, b):
    return lax.dot_general(a, b, (((1,), (1,)), ((), ())), preferred_element_type=F32)


def _split_bf16(x):
    hi = x.astype(BF16)
    lo = (x - hi.astype(F32)).astype(BF16)
    return hi, lo


def _in_proj_kernel(att_w, x_ref, w_ref, q_ref, k_ref, v_ref, glu_ref):
    z = jnp.dot(x_ref[...].astype(BF16), w_ref[...], preferred_element_type=F32)
    q_ref[...] = z[:, :att_w]
    k_ref[...] = z[:, att_w:2 * att_w]
    v_ref[...] = z[:, 2 * att_w:3 * att_w]
    conv_ch = (z.shape[1] - 3 * att_w) // 2
    a = z[:, 3 * att_w:3 * att_w + conv_ch]
    g = z[:, 3 * att_w + conv_ch:]
    glu_ref[...] = a * (1.0 / (1.0 + jnp.exp(-g)))


def _in_proj(x, w_in, att_w, tm):
    n, d = x.shape
    cols = w_in.shape[1]
    conv_ch = (cols - 3 * att_w) // 2
    out = lambda w: jax.ShapeDtypeStruct((n, w), F32)
    ospec = lambda w: pl.BlockSpec((tm, w), lambda i: (i, 0))
    return pl.pallas_call(
        functools.partial(_in_proj_kernel, att_w),
        grid=(n // tm,),
        in_specs=[pl.BlockSpec((tm, d), lambda i: (i, 0)),
                  pl.BlockSpec((d, cols), lambda i: (0, 0))],
        out_specs=[ospec(att_w), ospec(att_w), ospec(att_w), ospec(conv_ch)],
        out_shape=[out(att_w), out(att_w), out(att_w), out(conv_ch)],
        compiler_params=_cparams(("arbitrary",)),
        name="in_proj",
    )(x, w_in)


Q_BLOCK = 128
ATTN_UNROLL = 8
COPY_ROWS = 256


def _attn_prompt_kernel(q_ref, k_ref, v_ref, o_ref, qs, ks, vs, on, mn, ln, ob, mb, lb):
    seq = q_ref.shape[0]
    scale = HEAD_DIM ** -0.5
    lane = lax.broadcasted_iota(jnp.int32, (1, LANES), 1)
    first_head = lane < HEAD_DIM
    qi = lax.broadcasted_iota(jnp.int32, (Q_BLOCK, 2 * Q_BLOCK), 0)
    kj = lax.broadcasted_iota(jnp.int32, (Q_BLOCK, 2 * Q_BLOCK), 1)
    rel = qi - kj

    for d in DILATIONS:
        n_sub = seq // d
        blocks_per_sub = n_sub // Q_BLOCK
        for r in range(d):
            for c in range(n_sub // COPY_ROWS):
                src = pl.ds(r + d * COPY_ROWS * c, COPY_ROWS, stride=d) if d > 1 else pl.ds(COPY_ROWS * c, COPY_ROWS)
                dst = pl.ds(r * n_sub + COPY_ROWS * c, COPY_ROWS)
                qs[dst, :] = (q_ref[src, :] * scale).astype(BF16)
                ks[dst, :] = k_ref[src, :].astype(BF16)
                vs[dst, :] = v_ref[src, :].astype(BF16)
        o_dst, m_dst, l_dst = (on, mn, ln) if d == 1 else (ob, mb, lb)

        def one_block(i):
            il = i % blocks_per_sub
            has_prev = il > 0
            row0 = pl.multiple_of(i * Q_BLOCK, Q_BLOCK)
            kstart = pl.multiple_of(jnp.where(has_prev, row0 - Q_BLOCK, row0), Q_BLOCK)
            delta = rel + jnp.where(has_prev, Q_BLOCK, 0)
            valid = (delta >= 0) & (delta <= WIN_STEPS)
            qb = qs[pl.ds(row0, Q_BLOCK), :]
            kb = ks[pl.ds(kstart, 2 * Q_BLOCK), :]
            vb = vs[pl.ds(kstart, 2 * Q_BLOCK), :]
            outs = []
            for head_mask in (first_head, jnp.logical_not(first_head)):
                qh = jnp.where(head_mask, qb, jnp.zeros_like(qb))
                s = jnp.where(valid, _nt_dot(qh, kb), -jnp.inf)
                m = jnp.max(s, axis=1, keepdims=True)
                p = jnp.exp(s - m)
                l = jnp.sum(p, axis=1, keepdims=True)
                o = jnp.dot(p.astype(BF16), vb, preferred_element_type=F32)
                outs.append((o, m, l))
            (o0, m0, l0), (o1, m1, l1) = outs
            rows = pl.ds(row0, Q_BLOCK)
            o_dst[rows, :] = jnp.where(first_head, o0, o1)
            m_dst[rows, :] = jnp.where(first_head, m0, m1)
            l_dst[rows, :] = jnp.where(first_head, l0, l1)

        def blocks(it, carry):
            for u in range(ATTN_UNROLL):
                one_block(it * ATTN_UNROLL + u)
            return carry

        lax.fori_loop(0, seq // (Q_BLOCK * ATTN_UNROLL), blocks, 0)

        if d > 1:
            for r in range(d):
                for c in range(n_sub // COPY_ROWS):
                    nat = pl.ds(r + d * COPY_ROWS * c, COPY_ROWS, stride=d)
                    sub = pl.ds(r * n_sub + COPY_ROWS * c, COPY_ROWS)
                    m1, m2 = mn[nat, :], mb[sub, :]
                    m = jnp.maximum(m1, m2)
                    a1, a2 = jnp.exp(m1 - m), jnp.exp(m2 - m)
                    on[nat, :] = on[nat, :] * a1 + ob[sub, :] * a2
                    ln[nat, :] = ln[nat, :] * a1 + lb[sub, :] * a2
                    mn[nat, :] = m

    for c in range(seq // COPY_ROWS):
        rows = pl.ds(COPY_ROWS * c, COPY_ROWS)
        o_ref[rows, :] = on[rows, :] / ln[rows, :]


def _attn_prompt(q, k, v, batch, seq):
    width = q.shape[1]
    spec = pl.BlockSpec((seq, LANES), lambda b, g: (b, g))
    f32_scr = pltpu.VMEM((seq, LANES), F32)
    bf_scr = pltpu.VMEM((seq, LANES), BF16)
    return pl.pallas_call(
        _attn_prompt_kernel,
        grid=(batch, width // LANES),
        in_specs=[spec, spec, spec],
        out_specs=spec,
        out_shape=jax.ShapeDtypeStruct(q.shape, F32),
        scratch_shapes=[bf_scr] * 3 + [f32_scr] * 6,
        compiler_params=_cparams(("arbitrary", "arbitrary")),
        name="attn_prompt",
    )(q, k, v)


KEY_PAD = 128


def _attn_sample_kernel(windows, q_ref, k_ref, v_ref, ck_ref, cv_ref, o_ref, nk_ref, nv_ref, kall, vall):
    t_new, width = q_ref.shape[1], q_ref.shape[2]
    lb = ck_ref.shape[1]
    heads = width // HEAD_DIM
    rows = heads * t_new
    scale = HEAD_DIM ** -0.5
    k_new, v_new = k_ref[0], v_ref[0]

    n_chunks = (lb - t_new) // COPY_ROWS
    for c in range(n_chunks):
        nk_ref[0, COPY_ROWS * c:COPY_ROWS * (c + 1), :] = ck_ref[0, t_new + COPY_ROWS * c:t_new + COPY_ROWS * (c + 1), :]
        nv_ref[0, COPY_ROWS * c:COPY_ROWS * (c + 1), :] = cv_ref[0, t_new + COPY_ROWS * c:t_new + COPY_ROWS * (c + 1), :]
    done = COPY_ROWS * n_chunks
    nk_ref[0, done:lb - t_new, :] = ck_ref[0, t_new + done:lb, :]
    nv_ref[0, done:lb - t_new, :] = cv_ref[0, t_new + done:lb, :]
    nk_ref[0, lb - t_new:lb, :] = k_new
    nv_ref[0, lb - t_new:lb, :] = v_new

    for c in range(lb // COPY_ROWS):
        sl = slice(COPY_ROWS * c, COPY_ROWS * (c + 1))
        kall[sl, :] = ck_ref[0, sl, :].astype(BF16)
        vall[sl, :] = cv_ref[0, sl, :].astype(BF16)
    pad = jnp.zeros((KEY_PAD - t_new, width), F32)
    kall[lb:lb + KEY_PAD, :] = jnp.concatenate([k_new, pad], axis=0).astype(BF16)
    vall[lb:lb + KEY_PAD, :] = jnp.concatenate([v_new, pad], axis=0).astype(BF16)

    q_rep = jnp.concatenate([q_ref[0] * scale] * heads, axis=0)
    r_head = lax.broadcasted_iota(jnp.int32, (rows, width), 0) // t_new
    l_head = lax.broadcasted_iota(jnp.int32, (rows, width), 1) // HEAD_DIM
    own = r_head == l_head
    qbd = jnp.where(own, q_rep, 0.0).astype(BF16)
    s = _nt_dot(qbd, kall[...])
    tok = lax.broadcasted_iota(jnp.int32, s.shape, 0) % t_new
    key = lax.broadcasted_iota(jnp.int32, s.shape, 1)
    dist = lb + tok - key
    cnt = jnp.zeros(s.shape, F32)
    for w, d in windows:
        cnt = cnt + ((dist >= 0) & (dist <= w) & (dist % d == 0)).astype(F32)
    s = jnp.where(cnt > 0, s, -jnp.inf)
    m = jnp.max(s, axis=1, keepdims=True)
    p = cnt * jnp.exp(s - m)
    l = jnp.sum(p, axis=1, keepdims=True)
    o = jnp.dot(p.astype(BF16), vall[...], preferred_element_type=F32) / l
    o = jnp.where(own, o, 0.0).reshape(heads, t_new, width)
    o_ref[0] = jnp.sum(o, axis=0)


def _attn_sample(q, k, v, cache_k, cache_v, windows):
    batch, t_new, width = q.shape
    lb = cache_k.shape[1]
    new_spec = pl.BlockSpec((1, t_new, width), lambda b: (b, 0, 0))
    cache_spec = pl.BlockSpec((1, lb, width), lambda b: (b, 0, 0))
    return pl.pallas_call(
        functools.partial(_attn_sample_kernel, windows),
        grid=(batch,),
        in_specs=[new_spec, new_spec, new_spec, cache_spec, cache_spec],
        out_specs=[new_spec, cache_spec, cache_spec],
        out_shape=[jax.ShapeDtypeStruct(q.shape, F32), jax.ShapeDtypeStruct(cache_k.shape, F32),
                   jax.ShapeDtypeStruct(cache_v.shape, F32)],
        scratch_shapes=[pltpu.VMEM((lb + KEY_PAD, width), BF16)] * 2,
        compiler_params=_cparams(("arbitrary",)),
        name="attn_sample",
    )(q, k, v, cache_k, cache_v)


HALO = 32


def _layer_norm_rows(h, gain, bias):
    mu = jnp.mean(h, axis=-1, keepdims=True)
    hc = h - mu
    var = jnp.mean(hc * hc, axis=-1, keepdims=True)
    return hc * lax.rsqrt(var + LN_EPS) * gain + bias


def _mix_kernel(alpha, chunk, x_ref, att_ref, glu_ref, halo_ref, past_ref, wo_ref, cw_ref, cb_ref, cg_ref,
                cbeta_ref, g1_ref, b1_ref, x1_ref, xp, conv_scr):
    tm, conv_ch = glu_ref.shape
    first_tile = pl.program_id(1) == 0
    xp[0:HALO, :] = jnp.where(first_tile, past_ref[0], halo_ref[...])
    xp[HALO:HALO + tm, :] = glu_ref[...]
    lead = HALO - (CONV_K - 1)
    for rc in range(tm // chunk):
        acc = jnp.zeros((chunk, conv_ch), F32)
        for tap in range(CONV_K):
            acc = acc + cw_ref[tap:tap + 1, :] * xp[rc * chunk + lead + tap:rc * chunk + lead + tap + chunk, :]
        y = _layer_norm_rows(acc + cb_ref[...], cg_ref[...], cbeta_ref[...])
        conv_scr[rc * chunk:(rc + 1) * chunk, :] = y * (1.0 / (1.0 + jnp.exp(-y)))
    att_w = att_ref.shape[1]
    mix = jnp.dot(att_ref[...].astype(BF16), wo_ref[0:att_w, :], preferred_element_type=F32)
    mix = mix + jnp.dot(conv_scr[...].astype(BF16), wo_ref[att_w:, :], preferred_element_type=F32)
    x1_ref[...] = _layer_norm_rows(alpha * x_ref[...] + mix, g1_ref[...], b1_ref[...])


def _mix(x, att, glu, past, w_out, conv_w, conv_b, conv_g, conv_beta, g1, b1, alpha, batch, seq, tm):
    n, d = x.shape
    att_w, conv_ch = att.shape[1], glu.shape[1]
    tiles = seq // tm
    past = jnp.pad(past, ((0, 0), (HALO - (CONV_K - 1), 0), (0, 0)))
    row = lambda w: pl.BlockSpec((tm, w), lambda b, i: (b * tiles + i, 0))
    vec = lambda w: pl.BlockSpec((1, w), lambda b, i: (0, 0))
    halo_blocks = n // HALO
    halo = pl.BlockSpec((HALO, conv_ch),
                        lambda b, i: (jnp.clip((b * seq + i * tm) // HALO - 1, 0, halo_blocks - 1), 0))
    return pl.pallas_call(
        functools.partial(_mix_kernel, alpha, min(tm, 64)),
        grid=(batch, tiles),
        in_specs=[row(d), row(att_w), row(conv_ch), halo,
                  pl.BlockSpec((1, HALO, conv_ch), lambda b, i: (b, 0, 0)),
                  pl.BlockSpec(w_out.shape, lambda b, i: (0, 0)),
                  pl.BlockSpec(conv_w.shape, lambda b, i: (0, 0)),
                  vec(conv_ch), vec(conv_ch), vec(conv_ch), vec(d), vec(d)],
        out_specs=row(d),
        out_shape=jax.ShapeDtypeStruct((n, d), F32),
        scratch_shapes=[pltpu.VMEM((HALO + tm, conv_ch), F32), pltpu.VMEM((tm, conv_ch), F32)],
        compiler_params=_cparams(("arbitrary", "arbitrary")),
        name="mix",
    )(x, att, glu, glu, past, w_out, conv_w, conv_b.reshape(1, -1), conv_g.reshape(1, -1),
      conv_beta.reshape(1, -1), g1.reshape(1, -1), b1.reshape(1, -1))


def _extract_max(s, iota):
    m = jnp.max(s, axis=0, keepdims=True)
    idx = jnp.min(jnp.where(s == m, iota, float(s.shape[0])), axis=0, keepdims=True)
    return m, idx, jnp.where(iota == idx, -jnp.inf, s)


def _route_kernel(half_experts, x_ref, wq_ref, sk_ref, m_ref, hi_ref, gate_ref, v0, i0, v1, i1, bs, be, cs, ce):
    tt = x_ref.shape[0]
    q = jnp.dot(x_ref[...].astype(BF16), wq_ref[...], preferred_element_type=F32)
    qb = q.astype(BF16)
    s0 = _nt_dot(sk_ref[0, 0], qb[:, :LANES])
    s1 = _nt_dot(sk_ref[0, 1], qb[:, LANES:])
    iota_k = lax.broadcasted_iota(jnp.int32, (N_KEYS, tt), 0).astype(F32)

    def first(r, carry):
        a, b = carry
        ma, ia, a = _extract_max(a, iota_k)
        mb, ib, b = _extract_max(b, iota_k)
        v0[pl.ds(r, 1), :] = ma
        i0[pl.ds(r, 1), :] = ia
        v1[pl.ds(r, 1), :] = mb
        i1[pl.ds(r, 1), :] = ib
        return a, b

    lax.fori_loop(0, PEER_TOPK, first, (s0, s1))

    off = 0
    for a in range(PEER_TOPK):
        nb = PEER_TOPK // (a + 1)
        cs[off:off + nb, :] = v0[a:a + 1, :] + v1[0:nb, :]
        ce[off:off + nb, :] = i0[a:a + 1, :] * float(N_KEYS) + i1[0:nb, :]
        off += nb
    cs[off:, :] = jnp.full((cs.shape[0] - off, tt), -jnp.inf, F32)
    ce[off:, :] = jnp.full((cs.shape[0] - off, tt), -1.0, F32)
    cand_s, cand_e = cs[...], ce[...]
    iota_c = lax.broadcasted_iota(jnp.int32, cand_s.shape, 0).astype(F32)

    def second(r, c):
        m, idx, c = _extract_max(c, iota_c)
        bs[pl.ds(r, 1), :] = m
        be[pl.ds(r, 1), :] = jnp.max(jnp.where(iota_c == idx, cand_e, -1.0), axis=0, keepdims=True)
        return c

    lax.fori_loop(0, PEER_TOPK, second, cand_s)

    best = bs[...]
    p = jnp.exp(best - jnp.max(best, axis=0, keepdims=True))
    gate_ref[0] = p / jnp.sum(p, axis=0, keepdims=True)
    e = be[...].astype(jnp.int32)
    high = (e >= half_experts).astype(jnp.int32)
    hi_ref[0] = high
    m_ref[0] = (e - high * half_experts) * ROW_TILE


def _route(x, wq, sk, tt):
    n, d = x.shape
    half_experts = N_KEYS * N_KEYS // 2
    cand_rows = -(-sum(PEER_TOPK // (a + 1) for a in range(PEER_TOPK)) // ROW_TILE) * ROW_TILE
    out = jax.ShapeDtypeStruct((PEER_HEADS, PEER_TOPK, n), jnp.int32)
    ospec = pl.BlockSpec((1, PEER_TOPK, tt), lambda i, h: (h, 0, i))
    return pl.pallas_call(
        functools.partial(_route_kernel, half_experts),
        grid=(n // tt, PEER_HEADS),
        in_specs=[
            pl.BlockSpec((tt, d), lambda i, h: (i, 0)),
            pl.BlockSpec((d, 2 * LANES), lambda i, h: (0, h)),
            pl.BlockSpec((1, 2, N_KEYS, LANES), lambda i, h: (h, 0, 0, 0)),
        ],
        out_specs=[ospec, ospec, ospec],
        out_shape=[out, out, jax.ShapeDtypeStruct(out.shape, F32)],
        scratch_shapes=[pltpu.VMEM((PEER_TOPK, tt), F32)] * 6 + [pltpu.VMEM((cand_rows, tt), F32)] * 2,
        compiler_params=_cparams(("arbitrary", "arbitrary")),
        name="peer_route",
    )(x, wq, sk)


def _pack_table(tab):
    n, d = tab.shape
    half = n // 2
    t = tab.reshape(2, half, d // LANES, LANES).transpose(1, 0, 2, 3)
    t = t.reshape(half, ROW_TILE, 2, LANES).astype(BF16).transpose(0, 1, 3, 2)
    return lax.bitcast_convert_type(t, jnp.int32).reshape(half * ROW_TILE, LANES)


def _load_tile(tab_ref, row):
    return pltpu.bitcast(tab_ref[pl.ds(pl.multiple_of(row, ROW_TILE), ROW_TILE), :], BF16)


def _load_chunk(tab_ref, m_ref, base):
    tiles = []
    for j in range(CHUNK // 2):
        w = m_ref[base + j]
        tiles.append(_load_tile(tab_ref, w & 0xFFFF))
        tiles.append(_load_tile(tab_ref, lax.shift_right_logical(w, 16)))
    return tiles


def _diag_mask(rows):
    r = lax.broadcasted_iota(jnp.int32, (rows, CHUNK * PACK_ROWS), 0)
    c = lax.broadcasted_iota(jnp.int32, (rows, CHUNK * PACK_ROWS), 1)
    return (r % PACK_ROWS) == (c % PACK_ROWS)


def _peer_act_kernel(m_ref, x_ref, hi_ref, gate_ref, sel_ref, tab_ref, coef_ref, r_scr):
    tb = x_ref.shape[0]
    diag = _diag_mask(2 * PACK_ROWS)

    def tokens(it, carry):
        for u in range(TOKEN_UNROLL):
            t = it * TOKEN_UNROLL + u
            xh, xl = _split_bf16(x_ref[t])
            lhs = jnp.concatenate([xh, xh, xl, xl] * ACT_LHS_COPIES, axis=0)
            for c in range(SLOTS // CHUNK):
                tiles = _load_chunk(tab_ref, m_ref, t * (SLOTS // 2) + c * (CHUNK // 2))
                g = jnp.concatenate(tiles, axis=0)
                o = jnp.where(diag, _nt_dot(lhs, g)[:2 * PACK_ROWS], 0.0)
                r_scr[pl.ds(t, 1), c * 256:(c + 1) * 256] = jnp.sum(o, axis=0, keepdims=True)
        return carry

    lax.fori_loop(0, tb // TOKEN_UNROLL, tokens, 0)
    act2 = jnp.dot(r_scr[...], sel_ref[...], preferred_element_type=F32,
                   precision=lax.Precision.HIGHEST)
    act = jnp.where(hi_ref[...] > 0, act2[:, SLOTS:], act2[:, :SLOTS])
    gelu = 0.5 * act * (1.0 + lax.erf(act * (1.0 / math.sqrt(2.0))))
    coef_ref[...] = gate_ref[...] * gelu


def _peer_out_kernel(alpha, m_ref, x_ref, hi_ref, coef_ref, exp_ref, g_ref, b_ref, tab_ref, y_ref, ce_scr):
    tb = x_ref.shape[0]
    diag = _diag_mask(PACK_ROWS)
    coef = coef_ref[...]
    high = hi_ref[...] > 0
    c2 = jnp.concatenate([jnp.where(high, 0.0, coef), jnp.where(high, coef, 0.0)], axis=1)
    ce_scr[...] = jnp.dot(c2, exp_ref[...], preferred_element_type=F32,
                          precision=lax.Precision.HIGHEST)
    def tokens(it, carry):
        for u in range(TOKEN_UNROLL):
            t = it * TOKEN_UNROLL + u
            acc = jnp.zeros((PACK_ROWS, LANES), F32)
            for c in range(SLOTS // CHUNK):
                tiles = _load_chunk(tab_ref, m_ref, t * (SLOTS // 2) + c * (CHUNK // 2))
                g = jnp.concatenate(tiles, axis=0)
                ce = ce_scr[pl.ds(t, 1), c * 256:(c + 1) * 256]
                cm = jnp.where(diag, jnp.broadcast_to(ce, (PACK_ROWS, 256)), 0.0)
                ch, cl = _split_bf16(cm)
                lhs = jnp.concatenate([ch, cl] * OUT_LHS_COPIES, axis=0)
                o = jnp.dot(lhs, g, preferred_element_type=F32)[:2 * PACK_ROWS]
                acc = acc + o[:PACK_ROWS] + o[PACK_ROWS:]
            y_ref[t] = alpha * x_ref[t] + acc[:ROW_TILE] + acc[ROW_TILE:]
        return carry

    lax.fori_loop(0, tb // TOKEN_UNROLL, tokens, 0)

    h = y_ref[...]
    inv_d = 1.0 / (ROW_TILE * LANES)
    total = lambda a: jnp.sum(jnp.sum(a, axis=2, keepdims=True), axis=1, keepdims=True)
    hc = h - total(h) * inv_d
    var = total(hc * hc) * inv_d
    y_ref[...] = hc * lax.rsqrt(var + LN_EPS) * g_ref[...] + b_ref[...]


def _exp_matrix():
    return _sel_matrix().T


def _peer_out(m_flat, x3, hi, coef, tab, gain, bias, alpha, tb):
    n = x3.shape[0]
    expand = _exp_matrix()
    return pl.pallas_call(
        functools.partial(_peer_out_kernel, alpha),
        grid=(n // tb,),
        in_specs=[
            pl.BlockSpec((tb * SLOTS // 2,), lambda i: (i,), memory_space=pltpu.SMEM),
            pl.BlockSpec((tb, ROW_TILE, LANES), lambda i: (i, 0, 0)),
            pl.BlockSpec((tb, SLOTS), lambda i: (i, 0)),
            pl.BlockSpec((tb, SLOTS), lambda i: (i, 0)),
            pl.BlockSpec(expand.shape, lambda i: (0, 0)),
            pl.BlockSpec((ROW_TILE, LANES), lambda i: (0, 0)),
            pl.BlockSpec((ROW_TILE, LANES), lambda i: (0, 0)),
            pl.BlockSpec(tab.shape, lambda i: (0, 0), pipeline_mode=pl.Buffered(1)),
        ],
        out_specs=pl.BlockSpec((tb, ROW_TILE, LANES), lambda i: (i, 0, 0)),
        out_shape=jax.ShapeDtypeStruct((n, ROW_TILE, LANES), F32),
        scratch_shapes=[pltpu.VMEM((tb, SLOTS * PACK_ROWS), F32)],
        compiler_params=_cparams(("arbitrary",)),
        name="peer_out",
    )(m_flat, x3, hi, coef, expand, gain.reshape(ROW_TILE, LANES), bias.reshape(ROW_TILE, LANES), tab)


def _sel_matrix():
    k = jnp.arange(SLOTS * PACK_ROWS)
    slot, row = k // PACK_ROWS, k % PACK_ROWS
    col = jnp.where(row < ROW_TILE, slot, SLOTS + slot)
    return (col[:, None] == jnp.arange(2 * SLOTS)[None, :]).astype(F32)


def _peer_act(m_flat, x3, hi, gate, tab, tb):
    n = x3.shape[0]
    sel = _sel_matrix()
    return pl.pallas_call(
        _peer_act_kernel,
        grid=(n // tb,),
        in_specs=[
            pl.BlockSpec((tb * SLOTS // 2,), lambda i: (i,), memory_space=pltpu.SMEM),
            pl.BlockSpec((tb, ROW_TILE, LANES), lambda i: (i, 0, 0)),
            pl.BlockSpec((tb, SLOTS), lambda i: (i, 0)),
            pl.BlockSpec((tb, SLOTS), lambda i: (i, 0)),
            pl.BlockSpec(sel.shape, lambda i: (0, 0)),
            pl.BlockSpec(tab.shape, lambda i: (0, 0), pipeline_mode=pl.Buffered(1)),
        ],
        out_specs=pl.BlockSpec((tb, SLOTS), lambda i: (i, 0)),
        out_shape=jax.ShapeDtypeStruct((n, SLOTS), F32),
        scratch_shapes=[pltpu.VMEM((tb, SLOTS * PACK_ROWS), F32)],
        compiler_params=_cparams(("arbitrary",)),
        name="peer_act",
    )(m_flat, x3, hi, gate, sel, tab)


def _peer_layer(x1, wq, sk, u_tab, v_tab, gain, bias, alpha):
    n, d = x1.shape
    m, hi, gate = _route(x1, wq, sk, min(ROUTE_TILE, n))
    to_rows = lambda a: a.transpose(2, 0, 1).reshape(n, SLOTS)
    m_rows, hi, gate = to_rows(m), to_rows(hi), to_rows(gate)
    m_flat = (m_rows[:, 0::2] | (m_rows[:, 1::2] << 16)).reshape(n * SLOTS // 2)
    x3 = x1.reshape(n, ROW_TILE, LANES)
    coef = _peer_act(m_flat, x3, hi, gate, u_tab, PEER_TILE)
    y = _peer_out(m_flat, x3, hi, coef, v_tab, gain, bias, alpha, PEER_TILE)
    return y.reshape(n, d)


ROUTE_TILE = 512
PEER_TILE = 128
PROJ_TILE = 256
MIX_TILE = 256


def kernel(x_prompt, x_sample, cache_k_win, cache_v_win, state_conv, w_in, w_out, conv_w, conv_b, conv_ln_g,
           conv_ln_b, ln1_g, ln1_b, w_query, sub_keys, expert_u, expert_v, ln2_g, ln2_b):
    depth, d_model, in_cols = w_in.shape
    batch, seq, _ = x_prompt.shape
    dec_batch, dec_seq, _ = x_sample.shape
    conv_ch = conv_w.shape[2]
    att_w = (in_cols - 2 * conv_ch) // 3
    heads = att_w // HEAD_DIM
    lb = cache_k_win.shape[2]
    windows = tuple((WIN_STEPS * d, d) for d in DILATIONS)
    w_max = windows[-1][0]
    assert lb == w_max and seq >= w_max and d_model == ROW_TILE * LANES
    alpha = (2.0 * depth) ** 0.25

    hp = x_prompt.reshape(batch * seq, d_model)
    hs = x_sample.reshape(dec_batch * dec_seq, d_model)
    outs = [[] for _ in range(6)]
    for l in range(depth):
        w_in_l, w_out_l = w_in[l].astype(BF16), w_out[l].astype(BF16)
        wq_l, sk_l = w_query[l].astype(BF16), sub_keys[l].astype(BF16)
        u_tab, v_tab = _pack_table(expert_u[l]), _pack_table(expert_v[l])
        mix_args = (w_out_l, conv_w[l], conv_b[l], conv_ln_g[l], conv_ln_b[l], ln1_g[l], ln1_b[l], alpha)
        peer_args = (wq_l, sk_l, u_tab, v_tab, ln2_g[l], ln2_b[l], alpha)

        q, k, v, glu = _in_proj(hp, w_in_l, att_w, PROJ_TILE)
        att = _attn_prompt(q, k, v, batch, seq)
        past = jnp.zeros((batch, CONV_K - 1, conv_ch), F32)
        x1 = _mix(hp, att, glu, past, *mix_args, batch, seq, MIX_TILE)
        outs[0].append(k.reshape(batch, seq, heads, HEAD_DIM)[:, seq - w_max:])
        outs[1].append(v.reshape(batch, seq, heads, HEAD_DIM)[:, seq - w_max:])
        outs[2].append(glu.reshape(batch, seq, conv_ch)[:, seq - (CONV_K - 1):])
        hp = _peer_layer(x1, *peer_args)

        q, k, v, glu = _in_proj(hs, w_in_l, att_w, dec_batch * dec_seq)
        new3 = lambda a: a.reshape(dec_batch, dec_seq, att_w)
        att, nk, nv = _attn_sample(new3(q), new3(k), new3(v), cache_k_win[l].reshape(dec_batch, lb, att_w),
                                   cache_v_win[l].reshape(dec_batch, lb, att_w), windows)
        x1 = _mix(hs, att.reshape(dec_batch * dec_seq, att_w), glu, state_conv[l], *mix_args,
                  dec_batch, dec_seq, dec_seq)
        conv_in = jnp.concatenate([state_conv[l], glu.reshape(dec_batch, dec_seq, conv_ch)], axis=1)
        outs[3].append(nk.reshape(dec_batch, lb, heads, HEAD_DIM))
        outs[4].append(nv.reshape(dec_batch, lb, heads, HEAD_DIM))
        outs[5].append(conv_in[:, dec_seq:])
        hs = _peer_layer(x1, *peer_args)

    stack = lambda rows: jnp.stack(rows, 0)
    return (hp.reshape(batch, seq, d_model), hs.reshape(dec_batch, dec_seq, d_model),
            stack(outs[0]), stack(outs[1]), stack(outs[2]), stack(outs[3]), stack(outs[4]), stack(outs[5]))
```

```python
import functools
import math

import jax
import jax.numpy as jnp
from jax import lax
from jax.experimental import pallas as pl
from jax.experimental.pallas import tpu as pltpu

F32 = jnp.float32
BF16 = jnp.bfloat16

HEAD_DIM = 64
N_KEYS = 128
PEER_HEADS = 8
PEER_TOPK = 16
SLOTS = PEER_HEADS * PEER_TOPK
CONV_K = 31
DILATIONS = (1, 4, 16)
WIN_STEPS = 128
LN_EPS = 1e-5

LANES = 128
ROW_TILE = 8
PACK_ROWS = 16
CHUNK = 16
TOKEN_UNROLL = 16
ACT_LHS_COPIES = 4
OUT_LHS_COPIES = 3
VMEM_LIMIT = 56 * 1024 * 1024
PACK_TILE = 512


def _cparams(sem):
    return pltpu.CompilerParams(dimension_semantics=sem, vmem_limit_bytes=VMEM_LIMIT)


def _nt_dot(a, b):
    return lax.dot_general(a, b, (((1,), (1,)), ((), ())), preferred_element_type=F32)


def _split_bf16(x):
    hi = x.astype(BF16)
    lo = (x - hi.astype(F32)).astype(BF16)
    return hi, lo


def _dot_select(a, sel):
    hi = a.astype(BF16)
    rest = a - hi.astype(F32)
    mid = rest.astype(BF16)
    lo = (rest - mid.astype(F32)).astype(BF16)
    dot = lambda p: jnp.dot(p, sel, preferred_element_type=F32)
    return dot(hi) + dot(mid) + dot(lo)


def _in_proj_kernel(att_w, x_ref, w_ref, q_ref, k_ref, v_ref, glu_ref):
    z = jnp.dot(x_ref[...].astype(BF16), w_ref[...], preferred_element_type=F32)
    q_ref[...] = z[:, :att_w]
    k_ref[...] = z[:, att_w:2 * att_w]
    v_ref[...] = z[:, 2 * att_w:3 * att_w]
    conv_ch = (z.shape[1] - 3 * att_w) // 2
    a = z[:, 3 * att_w:3 * att_w + conv_ch]
    g = z[:, 3 * att_w + conv_ch:]
    glu_ref[...] = a * (1.0 / (1.0 + jnp.exp(-g)))


def _in_proj(x, w_in, att_w, tm):
    n, d = x.shape
    cols = w_in.shape[1]
    conv_ch = (cols - 3 * att_w) // 2
    out = lambda w: jax.ShapeDtypeStruct((n, w), F32)
    ospec = lambda w: pl.BlockSpec((tm, w), lambda i: (i, 0))
    return pl.pallas_call(
        functools.partial(_in_proj_kernel, att_w),
        grid=(n // tm,),
        in_specs=[pl.BlockSpec((tm, d), lambda i: (i, 0)),
                  pl.BlockSpec((d, cols), lambda i: (0, 0))],
        out_specs=[ospec(att_w), ospec(att_w), ospec(att_w), ospec(conv_ch)],
        out_shape=[out(att_w), out(att_w), out(att_w), out(conv_ch)],
        compiler_params=_cparams(("arbitrary",)),
        name="in_proj",
    )(x, w_in)


Q_BLOCK = 128
ATTN_UNROLL = 8
COPY_ROWS = 256


def _attn_prompt_kernel(q_ref, k_ref, v_ref, o_ref, qs, ks, vs, on, mn, ln, ob, mb, lb):
    seq = q_ref.shape[0]
    scale = HEAD_DIM ** -0.5
    lane = lax.broadcasted_iota(jnp.int32, (1, LANES), 1)
    first_head = lane < HEAD_DIM
    qi = lax.broadcasted_iota(jnp.int32, (Q_BLOCK, 2 * Q_BLOCK), 0)
    kj = lax.broadcasted_iota(jnp.int32, (Q_BLOCK, 2 * Q_BLOCK), 1)
    rel = qi - kj

    for d in DILATIONS:
        n_sub = seq // d
        blocks_per_sub = n_sub // Q_BLOCK
        for r in range(d):
            for c in range(n_sub // COPY_ROWS):
                src = pl.ds(r + d * COPY_ROWS * c, COPY_ROWS, stride=d) if d > 1 else pl.ds(COPY_ROWS * c, COPY_ROWS)
                dst = pl.ds(r * n_sub + COPY_ROWS * c, COPY_ROWS)
                qs[dst, :] = (q_ref[src, :] * scale).astype(BF16)
                ks[dst, :] = k_ref[src, :].astype(BF16)
                vs[dst, :] = v_ref[src, :].astype(BF16)
        o_dst, m_dst, l_dst = (on, mn, ln) if d == 1 else (ob, mb, lb)

        def one_block(i):
            il = i % blocks_per_sub
            has_prev = il > 0
            row0 = pl.multiple_of(i * Q_BLOCK, Q_BLOCK)
            kstart = pl.multiple_of(jnp.where(has_prev, row0 - Q_BLOCK, row0), Q_BLOCK)
            delta = rel + jnp.where(has_prev, Q_BLOCK, 0)
            valid = (delta >= 0) & (delta <= WIN_STEPS)
            qb = qs[pl.ds(row0, Q_BLOCK), :]
            kb = ks[pl.ds(kstart, 2 * Q_BLOCK), :]
            vb = vs[pl.ds(kstart, 2 * Q_BLOCK), :]
            outs = []
            for head_mask in (first_head, jnp.logical_not(first_head)):
                qh = jnp.where(head_mask, qb, jnp.zeros_like(qb))
                s = jnp.where(valid, _nt_dot(qh, kb), -jnp.inf)
                m = jnp.max(s, axis=1, keepdims=True)
                p = jnp.exp(s - m)
                l = jnp.sum(p, axis=1, keepdims=True)
                o = jnp.dot(p.astype(BF16), vb, preferred_element_type=F32)
                outs.append((o, m, l))
            (o0, m0, l0), (o1, m1, l1) = outs
            rows = pl.ds(row0, Q_BLOCK)
            o_dst[rows, :] = jnp.where(first_head, o0, o1)
            m_dst[rows, :] = jnp.where(first_head, m0, m1)
            l_dst[rows, :] = jnp.where(first_head, l0, l1)

        def blocks(it, carry):
            for u in range(ATTN_UNROLL):
                one_block(it * ATTN_UNROLL + u)
            return carry

        lax.fori_loop(0, seq // (Q_BLOCK * ATTN_UNROLL), blocks, 0)

        if d > 1:
            for r in range(d):
                for c in range(n_sub // COPY_ROWS):
                    nat = pl.ds(r + d * COPY_ROWS * c, COPY_ROWS, stride=d)
                    sub = pl.ds(r * n_sub + COPY_ROWS * c, COPY_ROWS)
                    m1, m2 = mn[nat, :], mb[sub, :]
                    m = jnp.maximum(m1, m2)
                    a1, a2 = jnp.exp(m1 - m), jnp.exp(m2 - m)
                    on[nat, :] = on[nat, :] * a1 + ob[sub, :] * a2
                    ln[nat, :] = ln[nat, :] * a1 + lb[sub, :] * a2
                    mn[nat, :] = m

    for c in range(seq // COPY_ROWS):
        rows = pl.ds(COPY_ROWS * c, COPY_ROWS)
        o_ref[rows, :] = on[rows, :] / ln[rows, :]


def _attn_prompt(q, k, v, batch, seq):
    width = q.shape[1]
    spec = pl.BlockSpec((seq, LANES), lambda b, g: (b, g))
    f32_scr = pltpu.VMEM((seq, LANES), F32)
    bf_scr = pltpu.VMEM((seq, LANES), BF16)
    return pl.pallas_call(
        _attn_prompt_kernel,
        grid=(batch, width // LANES),
        in_specs=[spec, spec, spec],
        out_specs=spec,
        out_shape=jax.ShapeDtypeStruct(q.shape, F32),
        scratch_shapes=[bf_scr] * 3 + [f32_scr] * 6,
        compiler_params=_cparams(("arbitrary", "arbitrary")),
        name="attn_prompt",
    )(q, k, v)


KEY_PAD = 128


def _attn_sample_kernel(windows, q_ref, k_ref, v_ref, ck_ref, cv_ref, o_ref, nk_ref, nv_ref, kall, vall):
    t_new, width = q_ref.shape[1], q_ref.shape[2]
    lb = ck_ref.shape[1]
    heads = width // HEAD_DIM
    rows = heads * t_new
    scale = HEAD_DIM ** -0.5
    k_new, v_new = k_ref[0], v_ref[0]

    n_chunks = (lb - t_new) // COPY_ROWS
    for c in range(n_chunks):
        nk_ref[0, COPY_ROWS * c:COPY_ROWS * (c + 1), :] = ck_ref[0, t_new + COPY_ROWS * c:t_new + COPY_ROWS * (c + 1), :]
        nv_ref[0, COPY_ROWS * c:COPY_ROWS * (c + 1), :] = cv_ref[0, t_new + COPY_ROWS * c:t_new + COPY_ROWS * (c + 1), :]
    done = COPY_ROWS * n_chunks
    nk_ref[0, done:lb - t_new, :] = ck_ref[0, t_new + done:lb, :]
    nv_ref[0, done:lb - t_new, :] = cv_ref[0, t_new + done:lb, :]
    nk_ref[0, lb - t_new:lb, :] = k_new
    nv_ref[0, lb - t_new:lb, :] = v_new

    for c in range(lb // COPY_ROWS):
        sl = slice(COPY_ROWS * c, COPY_ROWS * (c + 1))
        kall[sl, :] = ck_ref[0, sl, :].astype(BF16)
        vall[sl, :] = cv_ref[0, sl, :].astype(BF16)
    pad = jnp.zeros((KEY_PAD - t_new, width), F32)
    kall[lb:lb + KEY_PAD, :] = jnp.concatenate([k_new, pad], axis=0).astype(BF16)
    vall[lb:lb + KEY_PAD, :] = jnp.concatenate([v_new, pad], axis=0).astype(BF16)

    q_rep = jnp.concatenate([q_ref[0] * scale] * heads, axis=0)
    r_head = lax.broadcasted_iota(jnp.int32, (rows, width), 0) // t_new
    l_head = lax.broadcasted_iota(jnp.int32, (rows, width), 1) // HEAD_DIM
    own = r_head == l_head
    qbd = jnp.where(own, q_rep, 0.0).astype(BF16)
    s = _nt_dot(qbd, kall[...])
    tok = lax.broadcasted_iota(jnp.int32, s.shape, 0) % t_new
    key = lax.broadcasted_iota(jnp.int32, s.shape, 1)
    dist = lb + tok - key
    cnt = jnp.zeros(s.shape, F32)
    for w, d in windows:
        cnt = cnt + ((dist >= 0) & (dist <= w) & (dist % d == 0)).astype(F32)
    s = jnp.where(cnt > 0, s, -jnp.inf)
    m = jnp.max(s, axis=1, keepdims=True)
    p = cnt * jnp.exp(s - m)
    l = jnp.sum(p, axis=1, keepdims=True)
    o = jnp.dot(p.astype(BF16), vall[...], preferred_element_type=F32) / l
    o = jnp.where(own, o, 0.0).reshape(heads, t_new, width)
    o_ref[0] = jnp.sum(o, axis=0)


def _attn_sample(q, k, v, cache_k, cache_v, windows):
    batch, t_new, width = q.shape
    lb = cache_k.shape[1]
    new_spec = pl.BlockSpec((1, t_new, width), lambda b: (b, 0, 0))
    cache_spec = pl.BlockSpec((1, lb, width), lambda b: (b, 0, 0))
    return pl.pallas_call(
        functools.partial(_attn_sample_kernel, windows),
        grid=(batch,),
        in_specs=[new_spec, new_spec, new_spec, cache_spec, cache_spec],
        out_specs=[new_spec, cache_spec, cache_spec],
        out_shape=[jax.ShapeDtypeStruct(q.shape, F32), jax.ShapeDtypeStruct(cache_k.shape, F32),
                   jax.ShapeDtypeStruct(cache_v.shape, F32)],
        scratch_shapes=[pltpu.VMEM((lb + KEY_PAD, width), BF16)] * 2,
        compiler_params=_cparams(("arbitrary",)),
        name="attn_sample",
    )(q, k, v, cache_k, cache_v)


HALO = 32


def _layer_norm_rows(h, gain, bias):
    mu = jnp.mean(h, axis=-1, keepdims=True)
    hc = h - mu
    var = jnp.mean(hc * hc, axis=-1, keepdims=True)
    return hc * lax.rsqrt(var + LN_EPS) * gain + bias


def _mix_kernel(alpha, chunk, x_ref, att_ref, glu_ref, halo_ref, past_ref, wo_ref, cw_ref, cb_ref, cg_ref,
                cbeta_ref, g1_ref, b1_ref, x1_ref, xp, conv_scr):
    tm, conv_ch = glu_ref.shape
    first_tile = pl.program_id(1) == 0
    xp[0:HALO, :] = jnp.where(first_tile, past_ref[0], halo_ref[...])
    xp[HALO:HALO + tm, :] = glu_ref[...]
    lead = HALO - (CONV_K - 1)
    for rc in range(tm // chunk):
        acc = jnp.zeros((chunk, conv_ch), F32)
        for tap in range(CONV_K):
            acc = acc + cw_ref[tap:tap + 1, :] * xp[rc * chunk + lead + tap:rc * chunk + lead + tap + chunk, :]
        y = _layer_norm_rows(acc + cb_ref[...], cg_ref[...], cbeta_ref[...])
        conv_scr[rc * chunk:(rc + 1) * chunk, :] = y * (1.0 / (1.0 + jnp.exp(-y)))
    att_w = att_ref.shape[1]
    mix = jnp.dot(att_ref[...].astype(BF16), wo_ref[0:att_w, :], preferred_element_type=F32)
    mix = mix + jnp.dot(conv_scr[...].astype(BF16), wo_ref[att_w:, :], preferred_element_type=F32)
    x1_ref[...] = _layer_norm_rows(alpha * x_ref[...] + mix, g1_ref[...], b1_ref[...])


def _mix(x, att, glu, past, w_out, conv_w, conv_b, conv_g, conv_beta, g1, b1, alpha, batch, seq, tm):
    n, d = x.shape
    att_w, conv_ch = att.shape[1], glu.shape[1]
    tiles = seq // tm
    past = jnp.pad(past, ((0, 0), (HALO - (CONV_K - 1), 0), (0, 0)))
    row = lambda w: pl.BlockSpec((tm, w), lambda b, i: (b * tiles + i, 0))
    vec = lambda w: pl.BlockSpec((1, w), lambda b, i: (0, 0))
    halo_blocks = n // HALO
    halo = pl.BlockSpec((HALO, conv_ch),
                        lambda b, i: (jnp.clip((b * seq + i * tm) // HALO - 1, 0, halo_blocks - 1), 0))
    return pl.pallas_call(
        functools.partial(_mix_kernel, alpha, min(tm, 64)),
        grid=(batch, tiles),
        in_specs=[row(d), row(att_w), row(conv_ch), halo,
                  pl.BlockSpec((1, HALO, conv_ch), lambda b, i: (b, 0, 0)),
                  pl.BlockSpec(w_out.shape, lambda b, i: (0, 0)),
                  pl.BlockSpec(conv_w.shape, lambda b, i: (0, 0)),
                  vec(conv_ch), vec(conv_ch), vec(conv_ch), vec(d), vec(d)],
        out_specs=row(d),
        out_shape=jax.ShapeDtypeStruct((n, d), F32),
        scratch_shapes=[pltpu.VMEM((HALO + tm, conv_ch), F32), pltpu.VMEM((tm, conv_ch), F32)],
        compiler_params=_cparams(("arbitrary", "arbitrary")),
        name="mix",
    )(x, att, glu, glu, past, w_out, conv_w, conv_b.reshape(1, -1), conv_g.reshape(1, -1),
      conv_beta.reshape(1, -1), g1.reshape(1, -1), b1.reshape(1, -1))


def _scan_max(s_ref, rows_ref, prev):
    rows, tt = s_ref.shape
    prev = jnp.broadcast_to(prev, (ROW_TILE, tt))
    m = jnp.full((ROW_TILE, tt), -jnp.inf, F32)
    g = jnp.zeros((ROW_TILE, tt), F32)
    for j in range(rows // ROW_TILE):
        rs = slice(j * ROW_TILE, (j + 1) * ROW_TILE)
        sj = jnp.where(rows_ref[rs, :] == prev, -jnp.inf, s_ref[rs, :])
        s_ref[rs, :] = sj
        gt = sj > m
        m = jnp.where(gt, sj, m)
        g = jnp.where(gt, float(j), g)
    key = g * float(ROW_TILE) + rows_ref[0:ROW_TILE, :]
    top = jnp.max(m, axis=0, keepdims=True)
    idx = jnp.min(jnp.where(m == top, key, float(rows)), axis=0, keepdims=True)
    return top, idx


def _route_kernel(half_experts, x_ref, wq_ref, sk_ref, m_ref, hi_ref, gate_ref, v0, i0, v1, i1, bs, be, cs, ce,
                  sa, sb, key_rows):
    tt = x_ref.shape[0]
    q = jnp.dot(x_ref[...].astype(BF16), wq_ref[...], preferred_element_type=F32)
    qb = q.astype(BF16)
    sa[...] = _nt_dot(sk_ref[0, 0], qb[:, :LANES])
    sb[...] = _nt_dot(sk_ref[0, 1], qb[:, LANES:])
    key_rows[...] = lax.broadcasted_iota(jnp.int32, (N_KEYS, tt), 0).astype(F32)

    def first(r, prev):
        ma, ia = _scan_max(sa, key_rows, prev[0])
        mb, ib = _scan_max(sb, key_rows, prev[1])
        v0[pl.ds(r, 1), :] = ma
        i0[pl.ds(r, 1), :] = ia
        v1[pl.ds(r, 1), :] = mb
        i1[pl.ds(r, 1), :] = ib
        return ia, ib

    none = jnp.full((1, tt), -1.0, F32)
    lax.fori_loop(0, PEER_TOPK, first, (none, none))

    off = 0
    for a in range(PEER_TOPK):
        nb = PEER_TOPK // (a + 1)
        cs[off:off + nb, :] = v0[a:a + 1, :] + v1[0:nb, :]
        ce[off:off + nb, :] = i0[a:a + 1, :] * float(N_KEYS) + i1[0:nb, :]
        off += nb
    cs[off:, :] = jnp.full((cs.shape[0] - off, tt), -jnp.inf, F32)
    ce[off:, :] = jnp.full((cs.shape[0] - off, tt), -1.0, F32)
    pairs = PEER_TOPK // 2
    cand_rows = cs.shape[0]

    def second(r, prev):
        m, idx = _scan_max(cs, key_rows, prev)
        row = lax.shift_right_logical(r, 1) + pairs * (r & 1)
        bs[pl.ds(row, 1), :] = m
        picked = jnp.where(key_rows[0:cand_rows, :] == idx, ce[...], -1.0)
        be[pl.ds(row, 1), :] = jnp.max(picked, axis=0, keepdims=True)
        return idx

    lax.fori_loop(0, PEER_TOPK, second, none)

    best = bs[...]
    p = jnp.exp(best - jnp.max(best, axis=0, keepdims=True))
    gate = p / jnp.sum(p, axis=0, keepdims=True)
    e = be[...].astype(jnp.int32)
    high = (e >= half_experts).astype(jnp.int32)
    rows = (e - high * half_experts) * ROW_TILE
    words = rows[:pairs] | (rows[pairs:] << 16)
    tb = m_ref.shape[2]
    for k in range(tt // tb):
        lanes = slice(k * tb, (k + 1) * tb)
        gate_ref[k] = gate[:, lanes]
        hi_ref[k] = high[:, lanes]
        m_ref[k] = words[:, lanes]


def _route(x, wq, sk, tt, tb):
    n, d = x.shape
    half_experts = N_KEYS * N_KEYS // 2
    cand_rows = -(-sum(PEER_TOPK // (a + 1) for a in range(PEER_TOPK)) // ROW_TILE) * ROW_TILE
    out = jax.ShapeDtypeStruct((n // tb, SLOTS, tb), jnp.int32)
    ospec = pl.BlockSpec((tt // tb, PEER_TOPK, tb), lambda i, h: (i, h, 0))
    return pl.pallas_call(
        functools.partial(_route_kernel, half_experts),
        grid=(n // tt, PEER_HEADS),
        in_specs=[
            pl.BlockSpec((tt, d), lambda i, h: (i, 0)),
            pl.BlockSpec((d, 2 * LANES), lambda i, h: (0, h)),
            pl.BlockSpec((1, 2, N_KEYS, LANES), lambda i, h: (h, 0, 0, 0)),
        ],
        out_specs=[pl.BlockSpec((tt // tb, PEER_TOPK // 2, tb), lambda i, h: (i, h, 0)), ospec, ospec],
        out_shape=[jax.ShapeDtypeStruct((n // tb, SLOTS // 2, tb), jnp.int32), out,
                   jax.ShapeDtypeStruct(out.shape, F32)],
        scratch_shapes=([pltpu.VMEM((PEER_TOPK, tt), F32)] * 6 + [pltpu.VMEM((cand_rows, tt), F32)] * 2
                        + [pltpu.VMEM((N_KEYS, tt), F32)] * 3),
        compiler_params=_cparams(("arbitrary", "arbitrary")),
        name="peer_route",
    )(x, wq, sk)


def _bf16_bits(x):
    return pltpu.bitcast(x.astype(BF16).astype(F32), jnp.uint32)


def _pack_kernel(lo_ref, hi_ref, o_ref):
    half_rows = ROW_TILE // 2
    for i in range(lo_ref.shape[0] // ROW_TILE):
        src = slice(i * ROW_TILE, (i + 1) * ROW_TILE)
        for part, ref in enumerate((lo_ref, hi_ref)):
            for q in range(half_rows):
                a = _bf16_bits(ref[src, 2 * q * LANES:(2 * q + 1) * LANES])
                b = _bf16_bits(ref[src, (2 * q + 1) * LANES:(2 * q + 2) * LANES])
                word = b | lax.shift_right_logical(a, jnp.uint32(16))
                rows = pl.ds(i * ROW_TILE * ROW_TILE + part * half_rows + q, ROW_TILE, stride=ROW_TILE)
                o_ref[rows, :] = pltpu.bitcast(word, jnp.int32)


def _pack_table(tab, rows=PACK_TILE):
    n, d = tab.shape
    half = n // 2
    steps = half // rows
    return pl.pallas_call(
        _pack_kernel,
        grid=(steps,),
        in_specs=[pl.BlockSpec((rows, d), lambda i: (i, 0)),
                  pl.BlockSpec((rows, d), lambda i: (i + steps, 0))],
        out_specs=pl.BlockSpec((rows * ROW_TILE, LANES), lambda i: (i, 0)),
        out_shape=jax.ShapeDtypeStruct((half * ROW_TILE, LANES), jnp.int32),
        compiler_params=_cparams(("arbitrary",)),
        name="pack_table",
    )(tab, tab)


def _load_tile(tab_ref, row):
    return pltpu.bitcast(tab_ref[pl.ds(pl.multiple_of(row, ROW_TILE), ROW_TILE), :], BF16)


def _load_chunk(tab_ref, m_ref, c, t, tb):
    words = [m_ref[(c * (CHUNK // 2) + j) * tb + t] for j in range(CHUNK // 2)]
    return ([_load_tile(tab_ref, w & 0xFFFF) for w in words]
            + [_load_tile(tab_ref, lax.shift_right_logical(w, 16)) for w in words])


def _diag_mask(rows):
    r = lax.broadcasted_iota(jnp.int32, (rows, CHUNK * PACK_ROWS), 0)
    c = lax.broadcasted_iota(jnp.int32, (rows, CHUNK * PACK_ROWS), 1)
    return (r % PACK_ROWS) == (c % PACK_ROWS)


def _peer_act_kernel(m_ref, x_ref, hi_ref, gate_ref, sel_ref, tab_ref, coef_ref, r_scr):
    tb = x_ref.shape[0]
    diag = _diag_mask(2 * PACK_ROWS)

    def tokens(it, carry):
        for u in range(TOKEN_UNROLL):
            t = it * TOKEN_UNROLL + u
            xh, xl = _split_bf16(x_ref[t])
            lhs = jnp.concatenate([xh, xh, xl, xl] * ACT_LHS_COPIES, axis=0)
            for c in range(SLOTS // CHUNK):
                tiles = _load_chunk(tab_ref, m_ref, c, t, tb)
                g = jnp.concatenate(tiles, axis=0)
                o = jnp.where(diag, _nt_dot(lhs, g)[:2 * PACK_ROWS], 0.0)
                r_scr[pl.ds(t, 1), c * 256:(c + 1) * 256] = jnp.sum(o, axis=0, keepdims=True)
        return carry

    lax.fori_loop(0, tb // TOKEN_UNROLL, tokens, 0)
    act2 = _dot_select(r_scr[...], sel_ref[...])
    high = hi_ref[0].astype(F32).T > 0
    act = jnp.where(high, act2[:, SLOTS:], act2[:, :SLOTS])
    gelu = 0.5 * act * (1.0 + lax.erf(act * (1.0 / math.sqrt(2.0))))
    coef_ref[...] = gate_ref[0].T * gelu


def _peer_out_kernel(alpha, m_ref, x_ref, hi_ref, coef_ref, exp_ref, g_ref, b_ref, tab_ref, y_ref, ce_scr):
    tb = x_ref.shape[0]
    diag = _diag_mask(PACK_ROWS)
    coef = coef_ref[...]
    high = hi_ref[0].astype(F32).T > 0
    c2 = jnp.concatenate([jnp.where(high, 0.0, coef), jnp.where(high, coef, 0.0)], axis=1)
    ce_scr[...] = _dot_select(c2, exp_ref[...])
    def tokens(it, carry):
        for u in range(TOKEN_UNROLL):
            t = it * TOKEN_UNROLL + u
            acc = jnp.zeros((PACK_ROWS, LANES), F32)
            for c in range(SLOTS // CHUNK):
                tiles = _load_chunk(tab_ref, m_ref, c, t, tb)
                g = jnp.concatenate(tiles, axis=0)
                ce = ce_scr[pl.ds(t, 1), c * 256:(c + 1) * 256]
                cm = jnp.where(diag, jnp.broadcast_to(ce, (PACK_ROWS, 256)), 0.0)
                ch, cl = _split_bf16(cm)
                lhs = jnp.concatenate([ch, cl] * OUT_LHS_COPIES, axis=0)
                o = jnp.dot(lhs, g, preferred_element_type=F32)[:2 * PACK_ROWS]
                acc = acc + o[:PACK_ROWS] + o[PACK_ROWS:]
            y_ref[t] = alpha * x_ref[t] + acc[:ROW_TILE] + acc[ROW_TILE:]
        return carry

    lax.fori_loop(0, tb // TOKEN_UNROLL, tokens, 0)

    h = y_ref[...]
    inv_d = 1.0 / (ROW_TILE * LANES)
    total = lambda a: jnp.sum(jnp.sum(a, axis=2, keepdims=True), axis=1, keepdims=True)
    hc = h - total(h) * inv_d
    var = total(hc * hc) * inv_d
    y_ref[...] = hc * lax.rsqrt(var + LN_EPS) * g_ref[...] + b_ref[...]


def _exp_matrix():
    return _sel_matrix().T


def _peer_out(m_flat, x3, hi, coef, tab, gain, bias, alpha, tb):
    n = x3.shape[0]
    expand = _exp_matrix()
    return pl.pallas_call(
        functools.partial(_peer_out_kernel, alpha),
        grid=(n // tb,),
        in_specs=[
            pl.BlockSpec((tb * SLOTS // 2,), lambda i: (i,), memory_space=pltpu.SMEM),
            pl.BlockSpec((tb, ROW_TILE, LANES), lambda i: (i, 0, 0)),
            pl.BlockSpec((1, SLOTS, tb), lambda i: (i, 0, 0)),
            pl.BlockSpec((tb, SLOTS), lambda i: (i, 0)),
            pl.BlockSpec(expand.shape, lambda i: (0, 0)),
            pl.BlockSpec((ROW_TILE, LANES), lambda i: (0, 0)),
            pl.BlockSpec((ROW_TILE, LANES), lambda i: (0, 0)),
            pl.BlockSpec(tab.shape, lambda i: (0, 0), pipeline_mode=pl.Buffered(1)),
        ],
        out_specs=pl.BlockSpec((tb, ROW_TILE, LANES), lambda i: (i, 0, 0)),
        out_shape=jax.ShapeDtypeStruct((n, ROW_TILE, LANES), F32),
        scratch_shapes=[pltpu.VMEM((tb, SLOTS * PACK_ROWS), F32)],
        compiler_params=_cparams(("arbitrary",)),
        name="peer_out",
    )(m_flat, x3, hi, coef, expand, gain.reshape(ROW_TILE, LANES), bias.reshape(ROW_TILE, LANES), tab)


def _sel_matrix():
    k = jnp.arange(SLOTS * PACK_ROWS)
    slot, row = k // PACK_ROWS, k % PACK_ROWS
    col = jnp.where(row < ROW_TILE, slot, SLOTS + slot)
    return (col[:, None] == jnp.arange(2 * SLOTS)[None, :]).astype(BF16)


def _peer_act(m_flat, x3, hi, gate, tab, tb):
    n = x3.shape[0]
    sel = _sel_matrix()
    return pl.pallas_call(
        _peer_act_kernel,
        grid=(n // tb,),
        in_specs=[
            pl.BlockSpec((tb * SLOTS // 2,), lambda i: (i,), memory_space=pltpu.SMEM),
            pl.BlockSpec((tb, ROW_TILE, LANES), lambda i: (i, 0, 0)),
            pl.BlockSpec((1, SLOTS, tb), lambda i: (i, 0, 0)),
            pl.BlockSpec((1, SLOTS, tb), lambda i: (i, 0, 0)),
            pl.BlockSpec(sel.shape, lambda i: (0, 0)),
            pl.BlockSpec(tab.shape, lambda i: (0, 0), pipeline_mode=pl.Buffered(1)),
        ],
        out_specs=pl.BlockSpec((tb, SLOTS), lambda i: (i, 0)),
        out_shape=jax.ShapeDtypeStruct((n, SLOTS), F32),
        scratch_shapes=[pltpu.VMEM((tb, SLOTS * PACK_ROWS), F32)],
        compiler_params=_cparams(("arbitrary",)),
        name="peer_act",
    )(m_flat, x3, hi, gate, sel, tab)


def _peer_layer(x1, wq, sk, u_tab, v_tab, gain, bias, alpha):
    n, d = x1.shape
    m, hi, gate = _route(x1, wq, sk, min(ROUTE_TILE, n), PEER_TILE)
    m_flat = m.reshape(n * SLOTS // 2)
    x3 = x1.reshape(n, ROW_TILE, LANES)
    coef = _peer_act(m_flat, x3, hi, gate, u_tab, PEER_TILE)
    y = _peer_out(m_flat, x3, hi, coef, v_tab, gain, bias, alpha, PEER_TILE)
    return y.reshape(n, d)


ROUTE_TILE = 512
PEER_TILE = 128
PROJ_TILE = 256
MIX_TILE = 256


def kernel(x_prompt, x_sample, cache_k_win, cache_v_win, state_conv, w_in, w_out, conv_w, conv_b, conv_ln_g,
           conv_ln_b, ln1_g, ln1_b, w_query, sub_keys, expert_u, expert_v, ln2_g, ln2_b):
    depth, d_model, in_cols = w_in.shape
    batch, seq, _ = x_prompt.shape
    dec_batch, dec_seq, _ = x_sample.shape
    conv_ch = conv_w.shape[2]
    att_w = (in_cols - 2 * conv_ch) // 3
    heads = att_w // HEAD_DIM
    lb = cache_k_win.shape[2]
    windows = tuple((WIN_STEPS * d, d) for d in DILATIONS)
    w_max = windows[-1][0]
    assert lb == w_max and seq >= w_max and d_model == ROW_TILE * LANES
    alpha = (2.0 * depth) ** 0.25

    hp = x_prompt.reshape(batch * seq, d_model)
    hs = x_sample.reshape(dec_batch * dec_seq, d_model)
    outs = [[] for _ in range(6)]
    for l in range(depth):
        w_in_l, w_out_l = w_in[l].astype(BF16), w_out[l].astype(BF16)
        wq_l, sk_l = w_query[l].astype(BF16), sub_keys[l].astype(BF16)
        u_tab, v_tab = _pack_table(expert_u[l]), _pack_table(expert_v[l])
        mix_args = (w_out_l, conv_w[l], conv_b[l], conv_ln_g[l], conv_ln_b[l], ln1_g[l], ln1_b[l], alpha)
        peer_args = (wq_l, sk_l, u_tab, v_tab, ln2_g[l], ln2_b[l], alpha)

        q, k, v, glu = _in_proj(hp, w_in_l, att_w, PROJ_TILE)
        att = _attn_prompt(q, k, v, batch, seq)
        past = jnp.zeros((batch, CONV_K - 1, conv_ch), F32)
        x1 = _mix(hp, att, glu, past, *mix_args, batch, seq, MIX_TILE)
        outs[0].append(k.reshape(batch, seq, heads, HEAD_DIM)[:, seq - w_max:])
        outs[1].append(v.reshape(batch, seq, heads, HEAD_DIM)[:, seq - w_max:])
        outs[2].append(glu.reshape(batch, seq, conv_ch)[:, seq - (CONV_K - 1):])
        hp = _peer_layer(x1, *peer_args)

        q, k, v, glu = _in_proj(hs, w_in_l, att_w, dec_batch * dec_seq)
        new3 = lambda a: a.reshape(dec_batch, dec_seq, att_w)
        att, nk, nv = _attn_sample(new3(q), new3(k), new3(v), cache_k_win[l].reshape(dec_batch, lb, att_w),
                                   cache_v_win[l].reshape(dec_batch, lb, att_w), windows)
        x1 = _mix(hs, att.reshape(dec_batch * dec_seq, att_w), glu, state_conv[l], *mix_args,
                  dec_batch, dec_seq, dec_seq)
        conv_in = jnp.concatenate([state_conv[l], glu.reshape(dec_batch, dec_seq, conv_ch)], axis=1)
        outs[3].append(nk.reshape(dec_batch, lb, heads, HEAD_DIM))
        outs[4].append(nv.reshape(dec_batch, lb, heads, HEAD_DIM))
        outs[5].append(conv_in[:, dec_seq:])
        hs = _peer_layer(x1, *peer_args)

    stack = lambda rows: jnp.stack(rows, 0)
    return (hp.reshape(batch, seq, d_model), hs.reshape(dec_batch, dec_seq, d_model),
            stack(outs[0]), stack(outs[1]), stack(outs[2]), stack(outs[3]), stack(outs[4]), stack(outs[5]))
```

```python
import functools
import math

import jax
import jax.numpy as jnp
from jax import lax
from jax.experimental import pallas as pl
from jax.experimental.pallas import tpu as pltpu

F32 = jnp.float32
BF16 = jnp.bfloat16

HEAD_DIM = 64
N_KEYS = 128
PEER_HEADS = 8
PEER_TOPK = 16
SLOTS = PEER_HEADS * PEER_TOPK
CONV_K = 31
DILATIONS = (1, 4, 16)
WIN_STEPS = 128
LN_EPS = 1e-5

LANES = 128
ROW_TILE = 8
PACK_ROWS = 16
CHUNK = 16
TOKEN_UNROLL = 32
ACT_LHS_COPIES = 4
OUT_LHS_COPIES = 3
VMEM_LIMIT = 56 * 1024 * 1024
PACK_TILE = 512


def _cparams(sem):
    return pltpu.CompilerParams(dimension_semantics=sem, vmem_limit_bytes=VMEM_LIMIT)


def _nt_dot(a, b):
    return lax.dot_general(a, b, (((1,), (1,)), ((), ())), preferred_element_type=F32)


def _split_bf16(x):
    hi = x.astype(BF16)
    lo = (x - hi.astype(F32)).astype(BF16)
    return hi, lo


def _dot_select(a, sel):
    hi = a.astype(BF16)
    rest = a - hi.astype(F32)
    mid = rest.astype(BF16)
    lo = (rest - mid.astype(F32)).astype(BF16)
    dot = lambda p: jnp.dot(p, sel, preferred_element_type=F32)
    return dot(hi) + dot(mid) + dot(lo)


def _in_proj_kernel(att_w, x_ref, w_ref, q_ref, k_ref, v_ref, glu_ref):
    z = jnp.dot(x_ref[...].astype(BF16), w_ref[...], preferred_element_type=F32)
    q_ref[...] = z[:, :att_w]
    k_ref[...] = z[:, att_w:2 * att_w]
    v_ref[...] = z[:, 2 * att_w:3 * att_w]
    conv_ch = (z.shape[1] - 3 * att_w) // 2
    a = z[:, 3 * att_w:3 * att_w + conv_ch]
    g = z[:, 3 * att_w + conv_ch:]
    glu_ref[...] = a * (1.0 / (1.0 + jnp.exp(-g)))


def _in_proj(x, w_in, att_w, tm):
    n, d = x.shape
    cols = w_in.shape[1]
    conv_ch = (cols - 3 * att_w) // 2
    out = lambda w: jax.ShapeDtypeStruct((n, w), F32)
    ospec = lambda w: pl.BlockSpec((tm, w), lambda i: (i, 0))
    return pl.pallas_call(
        functools.partial(_in_proj_kernel, att_w),
        grid=(n // tm,),
        in_specs=[pl.BlockSpec((tm, d), lambda i: (i, 0)),
                  pl.BlockSpec((d, cols), lambda i: (0, 0))],
        out_specs=[ospec(att_w), ospec(att_w), ospec(att_w), ospec(conv_ch)],
        out_shape=[out(att_w), out(att_w), out(att_w), out(conv_ch)],
        compiler_params=_cparams(("arbitrary",)),
        name="in_proj",
    )(x, w_in)


Q_BLOCK = 128
ATTN_UNROLL = 8
COPY_ROWS = 256


def _attn_prompt_kernel(q_ref, k_ref, v_ref, o_ref, qs, ks, vs, on, mn, ln, ob, mb, lb):
    seq = q_ref.shape[0]
    scale = HEAD_DIM ** -0.5
    lane = lax.broadcasted_iota(jnp.int32, (1, LANES), 1)
    first_head = lane < HEAD_DIM
    qi = lax.broadcasted_iota(jnp.int32, (Q_BLOCK, 2 * Q_BLOCK), 0)
    kj = lax.broadcasted_iota(jnp.int32, (Q_BLOCK, 2 * Q_BLOCK), 1)
    rel = qi - kj

    for d in DILATIONS:
        n_sub = seq // d
        blocks_per_sub = n_sub // Q_BLOCK
        for r in range(d):
            for c in range(n_sub // COPY_ROWS):
                src = pl.ds(r + d * COPY_ROWS * c, COPY_ROWS, stride=d) if d > 1 else pl.ds(COPY_ROWS * c, COPY_ROWS)
                dst = pl.ds(r * n_sub + COPY_ROWS * c, COPY_ROWS)
                qs[dst, :] = (q_ref[src, :] * scale).astype(BF16)
                ks[dst, :] = k_ref[src, :].astype(BF16)
                vs[dst, :] = v_ref[src, :].astype(BF16)
        o_dst, m_dst, l_dst = (on, mn, ln) if d == 1 else (ob, mb, lb)

        def one_block(i):
            il = i % blocks_per_sub
            has_prev = il > 0
            row0 = pl.multiple_of(i * Q_BLOCK, Q_BLOCK)
            kstart = pl.multiple_of(jnp.where(has_prev, row0 - Q_BLOCK, row0), Q_BLOCK)
            delta = rel + jnp.where(has_prev, Q_BLOCK, 0)
            valid = (delta >= 0) & (delta <= WIN_STEPS)
            qb = qs[pl.ds(row0, Q_BLOCK), :]
            kb = ks[pl.ds(kstart, 2 * Q_BLOCK), :]
            vb = vs[pl.ds(kstart, 2 * Q_BLOCK), :]
            outs = []
            for head_mask in (first_head, jnp.logical_not(first_head)):
                qh = jnp.where(head_mask, qb, jnp.zeros_like(qb))
                s = jnp.where(valid, _nt_dot(qh, kb), -jnp.inf)
                m = jnp.max(s, axis=1, keepdims=True)
                p = jnp.exp(s - m)
                l = jnp.sum(p, axis=1, keepdims=True)
                o = jnp.dot(p.astype(BF16), vb, preferred_element_type=F32)
                outs.append((o, m, l))
            (o0, m0, l0), (o1, m1, l1) = outs
            rows = pl.ds(row0, Q_BLOCK)
            o_dst[rows, :] = jnp.where(first_head, o0, o1)
            m_dst[rows, :] = jnp.where(first_head, m0, m1)
            l_dst[rows, :] = jnp.where(first_head, l0, l1)

        def blocks(it, carry):
            for u in range(ATTN_UNROLL):
                one_block(it * ATTN_UNROLL + u)
            return carry

        lax.fori_loop(0, seq // (Q_BLOCK * ATTN_UNROLL), blocks, 0)

        if d > 1:
            for r in range(d):
                for c in range(n_sub // COPY_ROWS):
                    nat = pl.ds(r + d * COPY_ROWS * c, COPY_ROWS, stride=d)
                    sub = pl.ds(r * n_sub + COPY_ROWS * c, COPY_ROWS)
                    m1, m2 = mn[nat, :], mb[sub, :]
                    m = jnp.maximum(m1, m2)
                    a1, a2 = jnp.exp(m1 - m), jnp.exp(m2 - m)
                    on[nat, :] = on[nat, :] * a1 + ob[sub, :] * a2
                    ln[nat, :] = ln[nat, :] * a1 + lb[sub, :] * a2
                    mn[nat, :] = m

    for c in range(seq // COPY_ROWS):
        rows = pl.ds(COPY_ROWS * c, COPY_ROWS)
        o_ref[rows, :] = on[rows, :] / ln[rows, :]


def _attn_prompt(q, k, v, batch, seq):
    width = q.shape[1]
    spec = pl.BlockSpec((seq, LANES), lambda b, g: (b, g))
    f32_scr = pltpu.VMEM((seq, LANES), F32)
    bf_scr = pltpu.VMEM((seq, LANES), BF16)
    return pl.pallas_call(
        _attn_prompt_kernel,
        grid=(batch, width // LANES),
        in_specs=[spec, spec, spec],
        out_specs=spec,
        out_shape=jax.ShapeDtypeStruct(q.shape, F32),
        scratch_shapes=[bf_scr] * 3 + [f32_scr] * 6,
        compiler_params=_cparams(("arbitrary", "arbitrary")),
        name="attn_prompt",
    )(q, k, v)


KEY_PAD = 128
ROLL_ROWS = 64


def _attn_sample_kernel(windows, q_ref, k_ref, v_ref, ck_ref, cv_ref, o_ref, nk_ref, nv_ref):
    t_new, width = q_ref.shape[1], q_ref.shape[2]
    lb = ck_ref.shape[2]
    heads = width // HEAD_DIM
    rows = heads * t_new
    scale = HEAD_DIM ** -0.5
    k_new, v_new = k_ref[0], v_ref[0]
    pad = jnp.zeros((KEY_PAD - t_new, width), F32)

    lane = lax.broadcasted_iota(jnp.int32, (ROLL_ROWS, KEY_PAD), 1)
    for src, new, dst in ((ck_ref, k_new, nk_ref), (cv_ref, v_new, nv_ref)):
        tail = jnp.concatenate([pad, new], axis=0).T
        for c in range(width // ROLL_ROWS):
            rs = slice(ROLL_ROWS * c, ROLL_ROWS * (c + 1))
            y = pltpu.roll(src[0, rs, :], lb - t_new, 1)
            dst[0, rs, :lb - KEY_PAD] = y[:, :lb - KEY_PAD]
            dst[0, rs, lb - KEY_PAD:] = jnp.where(lane >= KEY_PAD - t_new, tail[rs, :], y[:, lb - KEY_PAD:])

    q_rep = jnp.concatenate([q_ref[0] * scale] * heads, axis=0)
    r_head = lax.broadcasted_iota(jnp.int32, (rows, width), 0) // t_new
    l_head = lax.broadcasted_iota(jnp.int32, (rows, width), 1) // HEAD_DIM
    own = r_head == l_head
    qbd = jnp.where(own, q_rep, 0.0).astype(BF16)
    k_pad = jnp.concatenate([k_new, pad], axis=0).astype(BF16)
    v_pad = jnp.concatenate([v_new, pad], axis=0).astype(BF16)
    s_cache = jnp.dot(qbd, ck_ref[0].astype(BF16), preferred_element_type=F32)
    s = jnp.concatenate([s_cache, _nt_dot(qbd, k_pad)], axis=1)
    tok = lax.broadcasted_iota(jnp.int32, s.shape, 0) % t_new
    key = lax.broadcasted_iota(jnp.int32, s.shape, 1)
    dist = lb + tok - key
    cnt = jnp.zeros(s.shape, F32)
    for w, d in windows:
        cnt = cnt + ((dist >= 0) & (dist <= w) & (dist % d == 0)).astype(F32)
    s = jnp.where(cnt > 0, s, -jnp.inf)
    m = jnp.max(s, axis=1, keepdims=True)
    p = (cnt * jnp.exp(s - m)).astype(BF16)
    l = jnp.sum(p.astype(F32), axis=1, keepdims=True)
    o = _nt_dot(p[:, :lb], cv_ref[0].astype(BF16)) + jnp.dot(p[:, lb:], v_pad, preferred_element_type=F32)
    o = jnp.where(own, o / l, 0.0).reshape(heads, t_new, width)
    o_ref[0] = jnp.sum(o, axis=0)


def _attn_sample(q, k, v, cache_kt, cache_vt, windows):
    batch, t_new, width = q.shape
    lb = cache_kt.shape[2]
    new_spec = pl.BlockSpec((1, t_new, width), lambda b: (b, 0, 0))
    cache_spec = pl.BlockSpec((1, width, lb), lambda b: (b, 0, 0))
    return pl.pallas_call(
        functools.partial(_attn_sample_kernel, windows),
        grid=(batch,),
        in_specs=[new_spec, new_spec, new_spec, cache_spec, cache_spec],
        out_specs=[new_spec, cache_spec, cache_spec],
        out_shape=[jax.ShapeDtypeStruct(q.shape, F32), jax.ShapeDtypeStruct(cache_kt.shape, F32),
                   jax.ShapeDtypeStruct(cache_vt.shape, F32)],
        compiler_params=_cparams(("arbitrary",)),
        name="attn_sample",
    )(q, k, v, cache_kt, cache_vt)


HALO = 32


def _layer_norm_rows(h, gain, bias):
    mu = jnp.mean(h, axis=-1, keepdims=True)
    hc = h - mu
    var = jnp.mean(hc * hc, axis=-1, keepdims=True)
    return hc * lax.rsqrt(var + LN_EPS) * gain + bias


def _mix_kernel(alpha, chunk, x_ref, att_ref, glu_ref, halo_ref, past_ref, wo_ref, cw_ref, cb_ref, cg_ref,
                cbeta_ref, g1_ref, b1_ref, x1_ref, xp, conv_scr):
    tm, conv_ch = glu_ref.shape
    first_tile = pl.program_id(1) == 0
    xp[0:HALO, :] = jnp.where(first_tile, past_ref[0], halo_ref[...])
    xp[HALO:HALO + tm, :] = glu_ref[...]
    lead = HALO - (CONV_K - 1)
    for rc in range(tm // chunk):
        acc = jnp.zeros((chunk, conv_ch), F32)
        for tap in range(CONV_K):
            acc = acc + cw_ref[tap:tap + 1, :] * xp[rc * chunk + lead + tap:rc * chunk + lead + tap + chunk, :]
        y = _layer_norm_rows(acc + cb_ref[...], cg_ref[...], cbeta_ref[...])
        conv_scr[rc * chunk:(rc + 1) * chunk, :] = y * (1.0 / (1.0 + jnp.exp(-y)))
    att_w = att_ref.shape[1]
    mix = jnp.dot(att_ref[...].astype(BF16), wo_ref[0:att_w, :], preferred_element_type=F32)
    mix = mix + jnp.dot(conv_scr[...].astype(BF16), wo_ref[att_w:, :], preferred_element_type=F32)
    x1_ref[...] = _layer_norm_rows(alpha * x_ref[...] + mix, g1_ref[...], b1_ref[...])


def _mix(x, att, glu, past, w_out, conv_w, conv_b, conv_g, conv_beta, g1, b1, alpha, batch, seq, tm):
    n, d = x.shape
    att_w, conv_ch = att.shape[1], glu.shape[1]
    tiles = seq // tm
    past = jnp.pad(past, ((0, 0), (HALO - (CONV_K - 1), 0), (0, 0)))
    row = lambda w: pl.BlockSpec((tm, w), lambda b, i: (b * tiles + i, 0))
    vec = lambda w: pl.BlockSpec((1, w), lambda b, i: (0, 0))
    halo_blocks = n // HALO
    halo = pl.BlockSpec((HALO, conv_ch),
                        lambda b, i: (jnp.clip((b * seq + i * tm) // HALO - 1, 0, halo_blocks - 1), 0))
    return pl.pallas_call(
        functools.partial(_mix_kernel, alpha, min(tm, 64)),
        grid=(batch, tiles),
        in_specs=[row(d), row(att_w), row(conv_ch), halo,
                  pl.BlockSpec((1, HALO, conv_ch), lambda b, i: (b, 0, 0)),
                  pl.BlockSpec(w_out.shape, lambda b, i: (0, 0)),
                  pl.BlockSpec(conv_w.shape, lambda b, i: (0, 0)),
                  vec(conv_ch), vec(conv_ch), vec(conv_ch), vec(d), vec(d)],
        out_specs=row(d),
        out_shape=jax.ShapeDtypeStruct((n, d), F32),
        scratch_shapes=[pltpu.VMEM((HALO + tm, conv_ch), F32), pltpu.VMEM((tm, conv_ch), F32)],
        compiler_params=_cparams(("arbitrary", "arbitrary")),
        name="mix",
    )(x, att, glu, glu, past, w_out, conv_w, conv_b.reshape(1, -1), conv_g.reshape(1, -1),
      conv_beta.reshape(1, -1), g1.reshape(1, -1), b1.reshape(1, -1))


def _scan_max(s_ref, rows_ref, prev):
    rows, tt = s_ref.shape
    prev = jnp.broadcast_to(prev, (ROW_TILE, tt))
    m = jnp.full((ROW_TILE, tt), -jnp.inf, F32)
    g = jnp.zeros((ROW_TILE, tt), F32)
    for j in range(rows // ROW_TILE):
        rs = slice(j * ROW_TILE, (j + 1) * ROW_TILE)
        sj = jnp.where(rows_ref[rs, :] == prev, -jnp.inf, s_ref[rs, :])
        s_ref[rs, :] = sj
        gt = sj > m
        m = jnp.where(gt, sj, m)
        g = jnp.where(gt, float(j), g)
    key = g * float(ROW_TILE) + rows_ref[0:ROW_TILE, :]
    top = jnp.max(m, axis=0, keepdims=True)
    idx = jnp.min(jnp.where(m == top, key, float(rows)), axis=0, keepdims=True)
    return top, idx


def _route_kernel(half_experts, x_ref, wq_ref, sk_ref, m_ref, hi_ref, gate_ref, v0, i0, v1, i1, bs, be, cs, ce,
                  sa, sb, key_rows):
    tt = x_ref.shape[0]
    q = jnp.dot(x_ref[...].astype(BF16), wq_ref[...], preferred_element_type=F32)
    qb = q.astype(BF16)
    sa[...] = _nt_dot(sk_ref[0, 0], qb[:, :LANES])
    sb[...] = _nt_dot(sk_ref[0, 1], qb[:, LANES:])
    key_rows[...] = lax.broadcasted_iota(jnp.int32, (N_KEYS, tt), 0).astype(F32)

    def first(r, prev):
        ma, ia = _scan_max(sa, key_rows, prev[0])
        mb, ib = _scan_max(sb, key_rows, prev[1])
        v0[pl.ds(r, 1), :] = ma
        i0[pl.ds(r, 1), :] = ia
        v1[pl.ds(r, 1), :] = mb
        i1[pl.ds(r, 1), :] = ib
        return ia, ib

    none = jnp.full((1, tt), -1.0, F32)
    lax.fori_loop(0, PEER_TOPK, first, (none, none))

    off = 0
    for a in range(PEER_TOPK):
        nb = PEER_TOPK // (a + 1)
        cs[off:off + nb, :] = v0[a:a + 1, :] + v1[0:nb, :]
        ce[off:off + nb, :] = i0[a:a + 1, :] * float(N_KEYS) + i1[0:nb, :]
        off += nb
    cs[off:, :] = jnp.full((cs.shape[0] - off, tt), -jnp.inf, F32)
    ce[off:, :] = jnp.full((cs.shape[0] - off, tt), -1.0, F32)
    pairs = PEER_TOPK // 2
    cand_rows = cs.shape[0]

    def second(r, prev):
        m, idx = _scan_max(cs, key_rows, prev)
        row = lax.shift_right_logical(r, 1) + pairs * (r & 1)
        bs[pl.ds(row, 1), :] = m
        picked = jnp.where(key_rows[0:cand_rows, :] == idx, ce[...], -1.0)
        be[pl.ds(row, 1), :] = jnp.max(picked, axis=0, keepdims=True)
        return idx

    lax.fori_loop(0, PEER_TOPK, second, none)

    best = bs[...]
    p = jnp.exp(best - jnp.max(best, axis=0, keepdims=True))
    gate = p / jnp.sum(p, axis=0, keepdims=True)
    e = be[...].astype(jnp.int32)
    high = (e >= half_experts).astype(jnp.int32)
    rows = (e - high * half_experts) * ROW_TILE
    words = rows[:pairs] | (rows[pairs:] << 16)
    tb = m_ref.shape[2]
    for k in range(tt // tb):
        lanes = slice(k * tb, (k + 1) * tb)
        gate_ref[k] = gate[:, lanes]
        hi_ref[k] = high[:, lanes]
        m_ref[k] = words[:, lanes]


def _route(x, wq, sk, tt, tb):
    n, d = x.shape
    half_experts = N_KEYS * N_KEYS // 2
    cand_rows = -(-sum(PEER_TOPK // (a + 1) for a in range(PEER_TOPK)) // ROW_TILE) * ROW_TILE
    out = jax.ShapeDtypeStruct((n // tb, SLOTS, tb), jnp.int32)
    ospec = pl.BlockSpec((tt // tb, PEER_TOPK, tb), lambda i, h: (i, h, 0))
    return pl.pallas_call(
        functools.partial(_route_kernel, half_experts),
        grid=(n // tt, PEER_HEADS),
        in_specs=[
            pl.BlockSpec((tt, d), lambda i, h: (i, 0)),
            pl.BlockSpec((d, 2 * LANES), lambda i, h: (0, h)),
            pl.BlockSpec((1, 2, N_KEYS, LANES), lambda i, h: (h, 0, 0, 0)),
        ],
        out_specs=[pl.BlockSpec((tt // tb, PEER_TOPK // 2, tb), lambda i, h: (i, h, 0)), ospec, ospec],
        out_shape=[jax.ShapeDtypeStruct((n // tb, SLOTS // 2, tb), jnp.int32), out,
                   jax.ShapeDtypeStruct(out.shape, F32)],
        scratch_shapes=([pltpu.VMEM((PEER_TOPK, tt), F32)] * 6 + [pltpu.VMEM((cand_rows, tt), F32)] * 2
                        + [pltpu.VMEM((N_KEYS, tt), F32)] * 3),
        compiler_params=_cparams(("arbitrary", "arbitrary")),
        name="peer_route",
    )(x, wq, sk)


def _bf16_bits(x):
    return pltpu.bitcast(x.astype(BF16).astype(F32), jnp.uint32)


def _pack_kernel(lo_ref, hi_ref, o_ref):
    half_rows = ROW_TILE // 2
    for i in range(lo_ref.shape[0] // ROW_TILE):
        src = slice(i * ROW_TILE, (i + 1) * ROW_TILE)
        for part, ref in enumerate((lo_ref, hi_ref)):
            for q in range(half_rows):
                a = _bf16_bits(ref[src, 2 * q * LANES:(2 * q + 1) * LANES])
                b = _bf16_bits(ref[src, (2 * q + 1) * LANES:(2 * q + 2) * LANES])
                word = b | lax.shift_right_logical(a, jnp.uint32(16))
                rows = pl.ds(i * ROW_TILE * ROW_TILE + part * half_rows + q, ROW_TILE, stride=ROW_TILE)
                o_ref[rows, :] = pltpu.bitcast(word, jnp.int32)


def _pack_table(tab, rows=PACK_TILE):
    n, d = tab.shape
    half = n // 2
    steps = half // rows
    return pl.pallas_call(
        _pack_kernel,
        grid=(steps,),
        in_specs=[pl.BlockSpec((rows, d), lambda i: (i, 0)),
                  pl.BlockSpec((rows, d), lambda i: (i + steps, 0))],
        out_specs=pl.BlockSpec((rows * ROW_TILE, LANES), lambda i: (i, 0)),
        out_shape=jax.ShapeDtypeStruct((half * ROW_TILE, LANES), jnp.int32),
        compiler_params=_cparams(("arbitrary",)),
        name="pack_table",
    )(tab, tab)


def _load_tile(tab_ref, row):
    return pltpu.bitcast(tab_ref[pl.ds(pl.multiple_of(row, ROW_TILE), ROW_TILE), :], BF16)


def _load_chunk(tab_ref, m_ref, c, t, tb):
    words = [m_ref[(c * (CHUNK // 2) + j) * tb + t] for j in range(CHUNK // 2)]
    return ([_load_tile(tab_ref, w & 0xFFFF) for w in words]
            + [_load_tile(tab_ref, lax.shift_right_logical(w, 16)) for w in words])


def _diag_mask(rows):
    r = lax.broadcasted_iota(jnp.int32, (rows, CHUNK * PACK_ROWS), 0)
    c = lax.broadcasted_iota(jnp.int32, (rows, CHUNK * PACK_ROWS), 1)
    return (r % PACK_ROWS) == (c % PACK_ROWS)


def _peer_act_kernel(m_ref, x_ref, hi_ref, gate_ref, sel_ref, tab_ref, coef_ref, r_scr):
    tb = x_ref.shape[0]
    diag = _diag_mask(2 * PACK_ROWS)

    def tokens(it, carry):
        for u in range(TOKEN_UNROLL):
            t = it * TOKEN_UNROLL + u
            xh, xl = _split_bf16(x_ref[t])
            lhs = jnp.concatenate([xh, xh, xl, xl] * ACT_LHS_COPIES, axis=0)
            for c in range(SLOTS // CHUNK):
                tiles = _load_chunk(tab_ref, m_ref, c, t, tb)
                g = jnp.concatenate(tiles, axis=0)
                o = jnp.where(diag, _nt_dot(lhs, g)[:2 * PACK_ROWS], 0.0)
                r_scr[pl.ds(t, 1), c * 256:(c + 1) * 256] = jnp.sum(o, axis=0, keepdims=True)
        return carry

    lax.fori_loop(0, tb // TOKEN_UNROLL, tokens, 0)
    act2 = _dot_select(r_scr[...], sel_ref[...])
    high = hi_ref[0].astype(F32).T > 0
    act = jnp.where(high, act2[:, SLOTS:], act2[:, :SLOTS])
    gelu = 0.5 * act * (1.0 + lax.erf(act * (1.0 / math.sqrt(2.0))))
    coef_ref[...] = gate_ref[0].T * gelu


def _peer_out_kernel(alpha, m_ref, x_ref, hi_ref, coef_ref, exp_ref, g_ref, b_ref, tab_ref, y_ref, ce_scr, h_scr):
    tb = x_ref.shape[0]
    diag = _diag_mask(PACK_ROWS)
    coef = coef_ref[...]
    high = hi_ref[0].astype(F32).T > 0
    c2 = jnp.concatenate([jnp.where(high, 0.0, coef), jnp.where(high, coef, 0.0)], axis=1)
    ce_scr[...] = _dot_select(c2, exp_ref[...])
    def tokens(it, carry):
        for u in range(TOKEN_UNROLL):
            t = it * TOKEN_UNROLL + u
            acc = jnp.zeros((PACK_ROWS, LANES), F32)
            for c in range(SLOTS // CHUNK):
                tiles = _load_chunk(tab_ref, m_ref, c, t, tb)
                g = jnp.concatenate(tiles, axis=0)
                ce = ce_scr[pl.ds(t, 1), c * 256:(c + 1) * 256]
                cm = jnp.where(diag, jnp.broadcast_to(ce, (PACK_ROWS, 256)), 0.0)
                ch, cl = _split_bf16(cm)
                lhs = jnp.concatenate([ch, cl] * OUT_LHS_COPIES, axis=0)
                o = jnp.dot(lhs, g, preferred_element_type=F32)[:2 * PACK_ROWS]
                acc = acc + o[:PACK_ROWS] + o[PACK_ROWS:]
            rows = pl.ds(pl.multiple_of(t * ROW_TILE, ROW_TILE), ROW_TILE)
            h_scr[rows, :] = alpha * x_ref[t] + acc[:ROW_TILE] + acc[ROW_TILE:]
        return carry

    lax.fori_loop(0, tb // TOKEN_UNROLL, tokens, 0)

    h = jnp.concatenate([h_scr[pl.ds(j, tb, stride=ROW_TILE), :] for j in range(ROW_TILE)], axis=1)
    y_ref[...] = _layer_norm_rows(h, g_ref[...], b_ref[...])


def _exp_matrix():
    return _sel_matrix().T


def _peer_out(m_flat, x3, hi, coef, tab, gain, bias, alpha, tb):
    n = x3.shape[0]
    d = ROW_TILE * LANES
    expand = _exp_matrix()
    return pl.pallas_call(
        functools.partial(_peer_out_kernel, alpha),
        grid=(n // tb,),
        in_specs=[
            pl.BlockSpec((tb * SLOTS // 2,), lambda i: (i,), memory_space=pltpu.SMEM),
            pl.BlockSpec((tb, ROW_TILE, LANES), lambda i: (i, 0, 0)),
            pl.BlockSpec((1, SLOTS, tb), lambda i: (i, 0, 0)),
            pl.BlockSpec((tb, SLOTS), lambda i: (i, 0)),
            pl.BlockSpec(expand.shape, lambda i: (0, 0)),
            pl.BlockSpec((1, d), lambda i: (0, 0)),
            pl.BlockSpec((1, d), lambda i: (0, 0)),
            pl.BlockSpec(tab.shape, lambda i: (0, 0), pipeline_mode=pl.Buffered(1)),
        ],
        out_specs=pl.BlockSpec((tb, d), lambda i: (i, 0)),
        out_shape=jax.ShapeDtypeStruct((n, d), F32),
        scratch_shapes=[pltpu.VMEM((tb, SLOTS * PACK_ROWS), F32), pltpu.VMEM((tb * ROW_TILE, LANES), F32)],
        compiler_params=_cparams(("arbitrary",)),
        name="peer_out",
    )(m_flat, x3, hi, coef, expand, gain.reshape(1, d), bias.reshape(1, d), tab)


def _sel_matrix():
    k = jnp.arange(SLOTS * PACK_ROWS)
    slot, row = k // PACK_ROWS, k % PACK_ROWS
    col = jnp.where(row < ROW_TILE, slot, SLOTS + slot)
    return (col[:, None] == jnp.arange(2 * SLOTS)[None, :]).astype(BF16)


def _peer_act(m_flat, x3, hi, gate, tab, tb):
    n = x3.shape[0]
    sel = _sel_matrix()
    return pl.pallas_call(
        _peer_act_kernel,
        grid=(n // tb,),
        in_specs=[
            pl.BlockSpec((tb * SLOTS // 2,), lambda i: (i,), memory_space=pltpu.SMEM),
            pl.BlockSpec((tb, ROW_TILE, LANES), lambda i: (i, 0, 0)),
            pl.BlockSpec((1, SLOTS, tb), lambda i: (i, 0, 0)),
            pl.BlockSpec((1, SLOTS, tb), lambda i: (i, 0, 0)),
            pl.BlockSpec(sel.shape, lambda i: (0, 0)),
            pl.BlockSpec(tab.shape, lambda i: (0, 0), pipeline_mode=pl.Buffered(1)),
        ],
        out_specs=pl.BlockSpec((tb, SLOTS), lambda i: (i, 0)),
        out_shape=jax.ShapeDtypeStruct((n, SLOTS), F32),
        scratch_shapes=[pltpu.VMEM((tb, SLOTS * PACK_ROWS), F32)],
        compiler_params=_cparams(("arbitrary",)),
        name="peer_act",
    )(m_flat, x3, hi, gate, sel, tab)


def _peer_layer(x1, wq, sk, u_tab, v_tab, gain, bias, alpha):
    n, d = x1.shape
    m, hi, gate = _route(x1, wq, sk, min(ROUTE_TILE, n), PEER_TILE)
    m_flat = m.reshape(n * SLOTS // 2)
    x3 = x1.reshape(n, ROW_TILE, LANES)
    coef = _peer_act(m_flat, x3, hi, gate, u_tab, PEER_TILE)
    return _peer_out(m_flat, x3, hi, coef, v_tab, gain, bias, alpha, PEER_TILE)


ROUTE_TILE = 512
PEER_TILE = 128
PROJ_TILE = 256
MIX_TILE = 256


def kernel(x_prompt, x_sample, cache_k_win, cache_v_win, state_conv, w_in, w_out, conv_w, conv_b, conv_ln_g,
           conv_ln_b, ln1_g, ln1_b, w_query, sub_keys, expert_u, expert_v, ln2_g, ln2_b):
    depth, d_model, in_cols = w_in.shape
    batch, seq, _ = x_prompt.shape
    dec_batch, dec_seq, _ = x_sample.shape
    conv_ch = conv_w.shape[2]
    att_w = (in_cols - 2 * conv_ch) // 3
    heads = att_w // HEAD_DIM
    lb = cache_k_win.shape[2]
    windows = tuple((WIN_STEPS * d, d) for d in DILATIONS)
    w_max = windows[-1][0]
    assert lb == w_max and seq >= w_max and d_model == ROW_TILE * LANES
    alpha = (2.0 * depth) ** 0.25

    hp = x_prompt.reshape(batch * seq, d_model)
    hs = x_sample.reshape(dec_batch * dec_seq, d_model)
    outs = [[] for _ in range(6)]
    for l in range(depth):
        w_in_l, w_out_l = w_in[l].astype(BF16), w_out[l].astype(BF16)
        wq_l, sk_l = w_query[l].astype(BF16), sub_keys[l].astype(BF16)
        u_tab, v_tab = _pack_table(expert_u[l]), _pack_table(expert_v[l])
        mix_args = (w_out_l, conv_w[l], conv_b[l], conv_ln_g[l], conv_ln_b[l], ln1_g[l], ln1_b[l], alpha)
        peer_args = (wq_l, sk_l, u_tab, v_tab, ln2_g[l], ln2_b[l], alpha)

        q, k, v, glu = _in_proj(hp, w_in_l, att_w, PROJ_TILE)
        att = _attn_prompt(q, k, v, batch, seq)
        past = jnp.zeros((batch, CONV_K - 1, conv_ch), F32)
        x1 = _mix(hp, att, glu, past, *mix_args, batch, seq, MIX_TILE)
        outs[0].append(k.reshape(batch, seq, heads, HEAD_DIM)[:, seq - w_max:])
        outs[1].append(v.reshape(batch, seq, heads, HEAD_DIM)[:, seq - w_max:])
        outs[2].append(glu.reshape(batch, seq, conv_ch)[:, seq - (CONV_K - 1):])
        hp = _peer_layer(x1, *peer_args)

        q, k, v, glu = _in_proj(hs, w_in_l, att_w, dec_batch * dec_seq)
        new3 = lambda a: a.reshape(dec_batch, dec_seq, att_w)
        to_feature_major = lambda c: c.transpose(0, 2, 3, 1).reshape(dec_batch, att_w, lb)
        att, nk, nv = _attn_sample(new3(q), new3(k), new3(v), to_feature_major(cache_k_win[l]),
                                   to_feature_major(cache_v_win[l]), windows)
        x1 = _mix(hs, att.reshape(dec_batch * dec_seq, att_w), glu, state_conv[l], *mix_args,
                  dec_batch, dec_seq, dec_seq)
        conv_in = jnp.concatenate([state_conv[l], glu.reshape(dec_batch, dec_seq, conv_ch)], axis=1)
        to_position_major = lambda c: c.reshape(dec_batch, heads, HEAD_DIM, lb).transpose(0, 3, 1, 2)
        outs[3].append(to_position_major(nk))
        outs[4].append(to_position_major(nv))
        outs[5].append(conv_in[:, dec_seq:])
        hs = _peer_layer(x1, *peer_args)

    stack = lambda rows: jnp.stack(rows, 0)
    return (hp.reshape(batch, seq, d_model), hs.reshape(dec_batch, dec_seq, d_model),
            stack(outs[0]), stack(outs[1]), stack(outs[2]), stack(outs[3]), stack(outs[4]), stack(outs[5]))
```

```python
import functools
import math

import jax
import jax.numpy as jnp
from jax import lax
from jax.experimental import pallas as pl
from jax.experimental.pallas import tpu as pltpu

F32 = jnp.float32
BF16 = jnp.bfloat16

HEAD_DIM = 64
N_KEYS = 128
PEER_HEADS = 8
PEER_TOPK = 16
SLOTS = PEER_HEADS * PEER_TOPK
CONV_K = 31
DILATIONS = (1, 4, 16)
WIN_STEPS = 128
LN_EPS = 1e-5

LANES = 128
ROW_TILE = 8
PACK_ROWS = 16
CHUNK = 16
TOKEN_UNROLL = 32
ACT_LHS_COPIES = 4
OUT_LHS_COPIES = 3
VMEM_LIMIT = 56 * 1024 * 1024
PACK_TILE = 512


def _cparams(sem):
    return pltpu.CompilerParams(dimension_semantics=sem, vmem_limit_bytes=VMEM_LIMIT)


def _nt_dot(a, b):
    return lax.dot_general(a, b, (((1,), (1,)), ((), ())), preferred_element_type=F32)


def _split_bf16(x):
    hi = x.astype(BF16)
    lo = (x - hi.astype(F32)).astype(BF16)
    return hi, lo


def _dot_select(a, sel):
    hi = a.astype(BF16)
    rest = a - hi.astype(F32)
    mid = rest.astype(BF16)
    lo = (rest - mid.astype(F32)).astype(BF16)
    dot = lambda p: jnp.dot(p, sel, preferred_element_type=F32)
    return dot(hi) + dot(mid) + dot(lo)


def _in_proj_kernel(att_w, window, x_ref, w_ref, q_ref, k_ref, v_ref, glu_ref, *kv_t_refs):
    z = jnp.dot(x_ref[...].astype(BF16), w_ref[...], preferred_element_type=F32)
    q_ref[...] = z[:, :att_w]
    k_ref[...] = z[:, att_w:2 * att_w]
    v_ref[...] = z[:, 2 * att_w:3 * att_w]
    if window is not None:
        tiles_per_seq, first = window
        kt_ref, vt_ref = kv_t_refs

        @pl.when(pl.program_id(0) % tiles_per_seq >= first)
        def _():
            kt_ref[0] = z[:, att_w:2 * att_w].T
            vt_ref[0] = z[:, 2 * att_w:3 * att_w].T
    conv_ch = (z.shape[1] - 3 * att_w) // 2
    a = z[:, 3 * att_w:3 * att_w + conv_ch]
    g = z[:, 3 * att_w + conv_ch:]
    glu_ref[...] = a * (1.0 / (1.0 + jnp.exp(-g)))


def _in_proj(x, w_in, att_w, tm, seq=None, keep=None):
    n, d = x.shape
    cols = w_in.shape[1]
    conv_ch = (cols - 3 * att_w) // 2
    out = lambda w: jax.ShapeDtypeStruct((n, w), F32)
    ospec = lambda w: pl.BlockSpec((tm, w), lambda i: (i, 0))
    out_specs = [ospec(att_w), ospec(att_w), ospec(att_w), ospec(conv_ch)]
    out_shape = [out(att_w), out(att_w), out(att_w), out(conv_ch)]
    window = None
    if keep is not None:
        tiles_per_seq, first = seq // tm, (seq - keep) // tm
        window = (tiles_per_seq, first)
        wspec = pl.BlockSpec((1, att_w, tm),
                             lambda i: (i // tiles_per_seq, 0, jnp.maximum(i % tiles_per_seq - first, 0)))
        out_specs += [wspec, wspec]
        out_shape += [jax.ShapeDtypeStruct((n // seq, att_w, keep), F32)] * 2
    return pl.pallas_call(
        functools.partial(_in_proj_kernel, att_w, window),
        grid=(n // tm,),
        in_specs=[pl.BlockSpec((tm, d), lambda i: (i, 0)),
                  pl.BlockSpec((d, cols), lambda i: (0, 0))],
        out_specs=out_specs,
        out_shape=out_shape,
        compiler_params=_cparams(("arbitrary",)),
        name="in_proj",
    )(x, w_in)


Q_BLOCK = 128
ATTN_UNROLL = 8
COPY_ROWS = 256


def _attn_prompt_kernel(q_ref, k_ref, v_ref, o_ref, qs, ks, vs, on, mn, ln, ob, mb, lb):
    seq = q_ref.shape[0]
    scale = HEAD_DIM ** -0.5
    lane = lax.broadcasted_iota(jnp.int32, (1, LANES), 1)
    first_head = lane < HEAD_DIM
    qi = lax.broadcasted_iota(jnp.int32, (Q_BLOCK, 2 * Q_BLOCK), 0)
    kj = lax.broadcasted_iota(jnp.int32, (Q_BLOCK, 2 * Q_BLOCK), 1)
    rel = qi - kj

    for d in DILATIONS:
        n_sub = seq // d
        blocks_per_sub = n_sub // Q_BLOCK
        for r in range(d):
            for c in range(n_sub // COPY_ROWS):
                src = pl.ds(r + d * COPY_ROWS * c, COPY_ROWS, stride=d) if d > 1 else pl.ds(COPY_ROWS * c, COPY_ROWS)
                dst = pl.ds(r * n_sub + COPY_ROWS * c, COPY_ROWS)
                qs[dst, :] = (q_ref[src, :] * scale).astype(BF16)
                ks[dst, :] = k_ref[src, :].astype(BF16)
                vs[dst, :] = v_ref[src, :].astype(BF16)
        o_dst, m_dst, l_dst = (on, mn, ln) if d == 1 else (ob, mb, lb)

        def one_block(i):
            il = i % blocks_per_sub
            has_prev = il > 0
            row0 = pl.multiple_of(i * Q_BLOCK, Q_BLOCK)
            kstart = pl.multiple_of(jnp.where(has_prev, row0 - Q_BLOCK, row0), Q_BLOCK)
            delta = rel + jnp.where(has_prev, Q_BLOCK, 0)
            valid = (delta >= 0) & (delta <= WIN_STEPS)
            qb = qs[pl.ds(row0, Q_BLOCK), :]
            kb = ks[pl.ds(kstart, 2 * Q_BLOCK), :]
            vb = vs[pl.ds(kstart, 2 * Q_BLOCK), :]
            outs = []
            for head_mask in (first_head, jnp.logical_not(first_head)):
                qh = jnp.where(head_mask, qb, jnp.zeros_like(qb))
                s = jnp.where(valid, _nt_dot(qh, kb), -jnp.inf)
                m = jnp.max(s, axis=1, keepdims=True)
                p = jnp.exp(s - m)
                l = jnp.sum(p, axis=1, keepdims=True)
                o = jnp.dot(p.astype(BF16), vb, preferred_element_type=F32)
                outs.append((o, m, l))
            (o0, m0, l0), (o1, m1, l1) = outs
            rows = pl.ds(row0, Q_BLOCK)
            o_dst[rows, :] = jnp.where(first_head, o0, o1)
            m_dst[rows, :] = jnp.where(first_head, m0, m1)
            l_dst[rows, :] = jnp.where(first_head, l0, l1)

        def blocks(it, carry):
            for u in range(ATTN_UNROLL):
                one_block(it * ATTN_UNROLL + u)
            return carry

        lax.fori_loop(0, seq // (Q_BLOCK * ATTN_UNROLL), blocks, 0)

        if d > 1:
            for r in range(d):
                for c in range(n_sub // COPY_ROWS):
                    nat = pl.ds(r + d * COPY_ROWS * c, COPY_ROWS, stride=d)
                    sub = pl.ds(r * n_sub + COPY_ROWS * c, COPY_ROWS)
                    m1, m2 = mn[nat, :], mb[sub, :]
                    m = jnp.maximum(m1, m2)
                    a1, a2 = jnp.exp(m1 - m), jnp.exp(m2 - m)
                    on[nat, :] = on[nat, :] * a1 + ob[sub, :] * a2
                    ln[nat, :] = ln[nat, :] * a1 + lb[sub, :] * a2
                    mn[nat, :] = m

    for c in range(seq // COPY_ROWS):
        rows = pl.ds(COPY_ROWS * c, COPY_ROWS)
        o_ref[rows, :] = on[rows, :] / ln[rows, :]


def _attn_prompt(q, k, v, batch, seq):
    width = q.shape[1]
    spec = pl.BlockSpec((seq, LANES), lambda b, g: (b, g))
    f32_scr = pltpu.VMEM((seq, LANES), F32)
    bf_scr = pltpu.VMEM((seq, LANES), BF16)
    return pl.pallas_call(
        _attn_prompt_kernel,
        grid=(batch, width // LANES),
        in_specs=[spec, spec, spec],
        out_specs=spec,
        out_shape=jax.ShapeDtypeStruct(q.shape, F32),
        scratch_shapes=[bf_scr] * 3 + [f32_scr] * 6,
        compiler_params=_cparams(("arbitrary", "arbitrary")),
        name="attn_prompt",
    )(q, k, v)


KEY_PAD = 128
ROLL_ROWS = 64


def _attn_sample_kernel(windows, q_ref, k_ref, v_ref, ck_ref, cv_ref, o_ref, nk_ref, nv_ref):
    t_new, width = q_ref.shape[1], q_ref.shape[2]
    lb = ck_ref.shape[2]
    heads = width // HEAD_DIM
    rows = heads * t_new
    scale = HEAD_DIM ** -0.5
    k_new, v_new = k_ref[0], v_ref[0]
    pad = jnp.zeros((KEY_PAD - t_new, width), F32)

    lane = lax.broadcasted_iota(jnp.int32, (ROLL_ROWS, KEY_PAD), 1)
    for src, new, dst in ((ck_ref, k_new, nk_ref), (cv_ref, v_new, nv_ref)):
        tail = jnp.concatenate([pad, new], axis=0).T
        for c in range(width // ROLL_ROWS):
            rs = slice(ROLL_ROWS * c, ROLL_ROWS * (c + 1))
            y = pltpu.roll(src[0, rs, :], lb - t_new, 1)
            dst[0, rs, :lb - KEY_PAD] = y[:, :lb - KEY_PAD]
            dst[0, rs, lb - KEY_PAD:] = jnp.where(lane >= KEY_PAD - t_new, tail[rs, :], y[:, lb - KEY_PAD:])

    q_rep = jnp.concatenate([q_ref[0] * scale] * heads, axis=0)
    r_head = lax.broadcasted_iota(jnp.int32, (rows, width), 0) // t_new
    l_head = lax.broadcasted_iota(jnp.int32, (rows, width), 1) // HEAD_DIM
    own = r_head == l_head
    qbd = jnp.where(own, q_rep, 0.0).astype(BF16)
    k_pad = jnp.concatenate([k_new, pad], axis=0).astype(BF16)
    v_pad = jnp.concatenate([v_new, pad], axis=0).astype(BF16)
    s_cache = jnp.dot(qbd, ck_ref[0].astype(BF16), preferred_element_type=F32)
    s = jnp.concatenate([s_cache, _nt_dot(qbd, k_pad)], axis=1)
    tok = lax.broadcasted_iota(jnp.int32, s.shape, 0) % t_new
    key = lax.broadcasted_iota(jnp.int32, s.shape, 1)
    dist = lb + tok - key
    cnt = jnp.zeros(s.shape, F32)
    for w, d in windows:
        cnt = cnt + ((dist >= 0) & (dist <= w) & (dist % d == 0)).astype(F32)
    s = jnp.where(cnt > 0, s, -jnp.inf)
    m = jnp.max(s, axis=1, keepdims=True)
    p = (cnt * jnp.exp(s - m)).astype(BF16)
    l = jnp.sum(p.astype(F32), axis=1, keepdims=True)
    o = _nt_dot(p[:, :lb], cv_ref[0].astype(BF16)) + jnp.dot(p[:, lb:], v_pad, preferred_element_type=F32)
    o = jnp.where(own, o / l, 0.0).reshape(heads, t_new, width)
    o_ref[0] = jnp.sum(o, axis=0)


def _attn_sample(q, k, v, cache_kt, cache_vt, windows):
    batch, t_new, width = q.shape
    lb = cache_kt.shape[2]
    new_spec = pl.BlockSpec((1, t_new, width), lambda b: (b, 0, 0))
    cache_spec = pl.BlockSpec((1, width, lb), lambda b: (b, 0, 0))
    return pl.pallas_call(
        functools.partial(_attn_sample_kernel, windows),
        grid=(batch,),
        in_specs=[new_spec, new_spec, new_spec, cache_spec, cache_spec],
        out_specs=[new_spec, cache_spec, cache_spec],
        out_shape=[jax.ShapeDtypeStruct(q.shape, F32), jax.ShapeDtypeStruct(cache_kt.shape, F32),
                   jax.ShapeDtypeStruct(cache_vt.shape, F32)],
        compiler_params=_cparams(("arbitrary",)),
        name="attn_sample",
    )(q, k, v, cache_kt, cache_vt)


HALO = 32


def _layer_norm_rows(h, gain, bias):
    mu = jnp.mean(h, axis=-1, keepdims=True)
    hc = h - mu
    var = jnp.mean(hc * hc, axis=-1, keepdims=True)
    return hc * lax.rsqrt(var + LN_EPS) * gain + bias


def _mix_kernel(alpha, chunk, x_ref, att_ref, glu_ref, halo_ref, past_ref, wo_ref, cw_ref, cb_ref, cg_ref,
                cbeta_ref, g1_ref, b1_ref, x1_ref, xp, conv_scr, shifted):
    tm, conv_ch = glu_ref.shape
    first_tile = pl.program_id(1) == 0
    xp[0:HALO, :] = jnp.where(first_tile, past_ref[0], halo_ref[...])
    xp[HALO:HALO + tm, :] = glu_ref[...]
    lead = HALO - (CONV_K - 1)
    span = shifted.shape[1]
    for s in range(1, ROW_TILE):
        for r0 in range(0, span, COPY_ROWS):
            r1 = min(r0 + COPY_ROWS, span)
            shifted[s, r0:r1, :] = xp[r0 + s:r1 + s, :]
    for rc in range(tm // chunk):
        acc = jnp.zeros((chunk, conv_ch), F32)
        for tap in range(CONV_K):
            s, base = (lead + tap) % ROW_TILE, rc * chunk + (lead + tap) // ROW_TILE * ROW_TILE
            rows = xp[base:base + chunk, :] if s == 0 else shifted[s, base:base + chunk, :]
            acc = acc + cw_ref[tap:tap + 1, :] * rows
        y = _layer_norm_rows(acc + cb_ref[...], cg_ref[...], cbeta_ref[...])
        conv_scr[rc * chunk:(rc + 1) * chunk, :] = y * (1.0 / (1.0 + jnp.exp(-y)))
    att_w = att_ref.shape[1]
    mix = jnp.dot(att_ref[...].astype(BF16), wo_ref[0:att_w, :], preferred_element_type=F32)
    mix = mix + jnp.dot(conv_scr[...].astype(BF16), wo_ref[att_w:, :], preferred_element_type=F32)
    x1_ref[...] = _layer_norm_rows(alpha * x_ref[...] + mix, g1_ref[...], b1_ref[...])


def _mix(x, att, glu, past, w_out, conv_w, conv_b, conv_g, conv_beta, g1, b1, alpha, batch, seq, tm):
    n, d = x.shape
    att_w, conv_ch = att.shape[1], glu.shape[1]
    tiles = seq // tm
    past = jnp.pad(past, ((0, 0), (HALO - (CONV_K - 1), 0), (0, 0)))
    row = lambda w: pl.BlockSpec((tm, w), lambda b, i: (b * tiles + i, 0))
    vec = lambda w: pl.BlockSpec((1, w), lambda b, i: (0, 0))
    halo_blocks = n // HALO
    halo = pl.BlockSpec((HALO, conv_ch),
                        lambda b, i: (jnp.clip((b * seq + i * tm) // HALO - 1, 0, halo_blocks - 1), 0))
    return pl.pallas_call(
        functools.partial(_mix_kernel, alpha, min(tm, 128)),
        grid=(batch, tiles),
        in_specs=[row(d), row(att_w), row(conv_ch), halo,
                  pl.BlockSpec((1, HALO, conv_ch), lambda b, i: (b, 0, 0)),
                  pl.BlockSpec(w_out.shape, lambda b, i: (0, 0)),
                  pl.BlockSpec(conv_w.shape, lambda b, i: (0, 0)),
                  vec(conv_ch), vec(conv_ch), vec(conv_ch), vec(d), vec(d)],
        out_specs=row(d),
        out_shape=jax.ShapeDtypeStruct((n, d), F32),
        scratch_shapes=[pltpu.VMEM((HALO + tm, conv_ch), F32), pltpu.VMEM((tm, conv_ch), F32),
                        pltpu.VMEM((ROW_TILE, HALO + tm - ROW_TILE, conv_ch), F32)],
        compiler_params=_cparams(("arbitrary", "arbitrary")),
        name="mix",
    )(x, att, glu, glu, past, w_out, conv_w, conv_b.reshape(1, -1), conv_g.reshape(1, -1),
      conv_beta.reshape(1, -1), g1.reshape(1, -1), b1.reshape(1, -1))


def _scan_max(s_ref, rows_ref, prev):
    rows, tt = s_ref.shape
    prev = jnp.broadcast_to(prev, (ROW_TILE, tt))
    m = jnp.full((ROW_TILE, tt), -jnp.inf, F32)
    g = jnp.zeros((ROW_TILE, tt), F32)
    for j in range(rows // ROW_TILE):
        rs = slice(j * ROW_TILE, (j + 1) * ROW_TILE)
        sj = jnp.where(rows_ref[rs, :] == prev, -jnp.inf, s_ref[rs, :])
        s_ref[rs, :] = sj
        gt = sj > m
        m = jnp.where(gt, sj, m)
        g = jnp.where(gt, float(j), g)
    key = g * float(ROW_TILE) + rows_ref[0:ROW_TILE, :]
    top = jnp.max(m, axis=0, keepdims=True)
    idx = jnp.min(jnp.where(m == top, key, float(rows)), axis=0, keepdims=True)
    return top, idx


def _route_kernel(half_experts, x_ref, wq_ref, sk_ref, m_ref, hi_ref, gate_ref, v0, i0, v1, i1, bs, be, cs, ce,
                  sa, sb, key_rows):
    tt = x_ref.shape[0]
    q = jnp.dot(x_ref[...].astype(BF16), wq_ref[...], preferred_element_type=F32)
    qb = q.astype(BF16)
    sa[...] = _nt_dot(sk_ref[0, 0], qb[:, :LANES])
    sb[...] = _nt_dot(sk_ref[0, 1], qb[:, LANES:])
    key_rows[...] = lax.broadcasted_iota(jnp.int32, (N_KEYS, tt), 0).astype(F32)

    def first(r, prev):
        ma, ia = _scan_max(sa, key_rows, prev[0])
        mb, ib = _scan_max(sb, key_rows, prev[1])
        v0[pl.ds(r, 1), :] = ma
        i0[pl.ds(r, 1), :] = ia
        v1[pl.ds(r, 1), :] = mb
        i1[pl.ds(r, 1), :] = ib
        return ia, ib

    none = jnp.full((1, tt), -1.0, F32)
    lax.fori_loop(0, PEER_TOPK, first, (none, none))

    off = 0
    for a in range(PEER_TOPK):
        nb = PEER_TOPK // (a + 1)
        cs[off:off + nb, :] = v0[a:a + 1, :] + v1[0:nb, :]
        ce[off:off + nb, :] = i0[a:a + 1, :] * float(N_KEYS) + i1[0:nb, :]
        off += nb
    cs[off:, :] = jnp.full((cs.shape[0] - off, tt), -jnp.inf, F32)
    ce[off:, :] = jnp.full((cs.shape[0] - off, tt), -1.0, F32)
    pairs = PEER_TOPK // 2
    cand_rows = cs.shape[0]

    def second(r, prev):
        m, idx = _scan_max(cs, key_rows, prev)
        row = lax.shift_right_logical(r, 1) + pairs * (r & 1)
        bs[pl.ds(row, 1), :] = m
        picked = jnp.where(key_rows[0:cand_rows, :] == idx, ce[...], -1.0)
        be[pl.ds(row, 1), :] = jnp.max(picked, axis=0, keepdims=True)
        return idx

    lax.fori_loop(0, PEER_TOPK, second, none)

    best = bs[...]
    p = jnp.exp(best - jnp.max(best, axis=0, keepdims=True))
    gate = p / jnp.sum(p, axis=0, keepdims=True)
    e = be[...].astype(jnp.int32)
    high = (e >= half_experts).astype(jnp.int32)
    rows = (e - high * half_experts) * ROW_TILE
    words = rows[:pairs] | (rows[pairs:] << 16)
    tb = m_ref.shape[2]
    for k in range(tt // tb):
        lanes = slice(k * tb, (k + 1) * tb)
        gate_ref[k] = gate[:, lanes]
        hi_ref[k] = high[:, lanes]
        m_ref[k] = words[:, lanes]


def _route(x, wq, sk, tt, tb):
    n, d = x.shape
    half_experts = N_KEYS * N_KEYS // 2
    cand_rows = -(-sum(PEER_TOPK // (a + 1) for a in range(PEER_TOPK)) // ROW_TILE) * ROW_TILE
    out = jax.ShapeDtypeStruct((n // tb, SLOTS, tb), jnp.int32)
    ospec = pl.BlockSpec((tt // tb, PEER_TOPK, tb), lambda i, h: (i, h, 0))
    return pl.pallas_call(
        functools.partial(_route_kernel, half_experts),
        grid=(n // tt, PEER_HEADS),
        in_specs=[
            pl.BlockSpec((tt, d), lambda i, h: (i, 0)),
            pl.BlockSpec((d, 2 * LANES), lambda i, h: (0, h)),
            pl.BlockSpec((1, 2, N_KEYS, LANES), lambda i, h: (h, 0, 0, 0)),
        ],
        out_specs=[pl.BlockSpec((tt // tb, PEER_TOPK // 2, tb), lambda i, h: (i, h, 0)), ospec, ospec],
        out_shape=[jax.ShapeDtypeStruct((n // tb, SLOTS // 2, tb), jnp.int32), out,
                   jax.ShapeDtypeStruct(out.shape, F32)],
        scratch_shapes=([pltpu.VMEM((PEER_TOPK, tt), F32)] * 6 + [pltpu.VMEM((cand_rows, tt), F32)] * 2
                        + [pltpu.VMEM((N_KEYS, tt), F32)] * 3),
        compiler_params=_cparams(("arbitrary", "arbitrary")),
        name="peer_route",
    )(x, wq, sk)


def _bf16_bits(x):
    return pltpu.bitcast(x.astype(BF16).astype(F32), jnp.uint32)


def _pack_kernel(lo_ref, hi_ref, o_ref):
    half_rows = ROW_TILE // 2
    for i in range(lo_ref.shape[0] // ROW_TILE):
        src = slice(i * ROW_TILE, (i + 1) * ROW_TILE)
        for part, ref in enumerate((lo_ref, hi_ref)):
            for q in range(half_rows):
                a = _bf16_bits(ref[src, 2 * q * LANES:(2 * q + 1) * LANES])
                b = _bf16_bits(ref[src, (2 * q + 1) * LANES:(2 * q + 2) * LANES])
                word = b | lax.shift_right_logical(a, jnp.uint32(16))
                rows = pl.ds(i * ROW_TILE * ROW_TILE + part * half_rows + q, ROW_TILE, stride=ROW_TILE)
                o_ref[rows, :] = pltpu.bitcast(word, jnp.int32)


def _pack_table(tab, rows=PACK_TILE):
    n, d = tab.shape
    half = n // 2
    steps = half // rows
    return pl.pallas_call(
        _pack_kernel,
        grid=(steps,),
        in_specs=[pl.BlockSpec((rows, d), lambda i: (i, 0)),
                  pl.BlockSpec((rows, d), lambda i: (i + steps, 0))],
        out_specs=pl.BlockSpec((rows * ROW_TILE, LANES), lambda i: (i, 0)),
        out_shape=jax.ShapeDtypeStruct((half * ROW_TILE, LANES), jnp.int32),
        compiler_params=_cparams(("arbitrary",)),
        name="pack_table",
    )(tab, tab)


def _load_tile(tab_ref, row):
    return pltpu.bitcast(tab_ref[pl.ds(pl.multiple_of(row, ROW_TILE), ROW_TILE), :], BF16)


def _load_chunk(tab_ref, m_ref, c, t, tb):
    words = [m_ref[(c * (CHUNK // 2) + j) * tb + t] for j in range(CHUNK // 2)]
    return ([_load_tile(tab_ref, w & 0xFFFF) for w in words]
            + [_load_tile(tab_ref, lax.shift_right_logical(w, 16)) for w in words])


def _diag_mask(rows):
    r = lax.broadcasted_iota(jnp.int32, (rows, CHUNK * PACK_ROWS), 0)
    c = lax.broadcasted_iota(jnp.int32, (rows, CHUNK * PACK_ROWS), 1)
    return (r % PACK_ROWS) == (c % PACK_ROWS)


def _peer_act_kernel(m_ref, x_ref, hi_ref, gate_ref, sel_ref, tab_ref, coef_ref, r_scr):
    tb = x_ref.shape[0]
    diag = _diag_mask(2 * PACK_ROWS)

    def tokens(it, carry):
        for u in range(TOKEN_UNROLL):
            t = it * TOKEN_UNROLL + u
            xh, xl = _split_bf16(x_ref[t])
            lhs = jnp.concatenate([xh, xh, xl, xl] * ACT_LHS_COPIES, axis=0)
            for c in range(SLOTS // CHUNK):
                tiles = _load_chunk(tab_ref, m_ref, c, t, tb)
                g = jnp.concatenate(tiles, axis=0)
                o = jnp.where(diag, _nt_dot(lhs, g)[:2 * PACK_ROWS], 0.0)
                r_scr[pl.ds(t, 1), c * 256:(c + 1) * 256] = jnp.sum(o, axis=0, keepdims=True)
        return carry

    lax.fori_loop(0, tb // TOKEN_UNROLL, tokens, 0)
    act2 = _dot_select(r_scr[...], sel_ref[...])
    high = hi_ref[0].astype(F32).T > 0
    act = jnp.where(high, act2[:, SLOTS:], act2[:, :SLOTS])
    gelu = 0.5 * act * (1.0 + lax.erf(act * (1.0 / math.sqrt(2.0))))
    coef_ref[...] = gate_ref[0].T * gelu


def _peer_out_kernel(alpha, m_ref, x_ref, hi_ref, coef_ref, exp_ref, g_ref, b_ref, tab_ref, y_ref, ce_scr, h_scr):
    tb = x_ref.shape[0]
    diag = _diag_mask(PACK_ROWS)
    coef = coef_ref[...]
    high = hi_ref[0].astype(F32).T > 0
    c2 = jnp.concatenate([jnp.where(high, 0.0, coef), jnp.where(high, coef, 0.0)], axis=1)
    ce_scr[...] = _dot_select(c2, exp_ref[...])
    def tokens(it, carry):
        for u in range(TOKEN_UNROLL):
            t = it * TOKEN_UNROLL + u
            acc = jnp.zeros((PACK_ROWS, LANES), F32)
            for c in range(SLOTS // CHUNK):
                tiles = _load_chunk(tab_ref, m_ref, c, t, tb)
                g = jnp.concatenate(tiles, axis=0)
                ce = ce_scr[pl.ds(t, 1), c * 256:(c + 1) * 256]
                cm = jnp.where(diag, jnp.broadcast_to(ce, (PACK_ROWS, 256)), 0.0)
                ch, cl = _split_bf16(cm)
                lhs = jnp.concatenate([ch, cl] * OUT_LHS_COPIES, axis=0)
                o = jnp.dot(lhs, g, preferred_element_type=F32)[:2 * PACK_ROWS]
                acc = acc + o[:PACK_ROWS] + o[PACK_ROWS:]
            rows = pl.ds(pl.multiple_of(t * ROW_TILE, ROW_TILE), ROW_TILE)
            h_scr[rows, :] = alpha * x_ref[t] + acc[:ROW_TILE] + acc[ROW_TILE:]
        return carry

    lax.fori_loop(0, tb // TOKEN_UNROLL, tokens, 0)

    h = jnp.concatenate([h_scr[pl.ds(j, tb, stride=ROW_TILE), :] for j in range(ROW_TILE)], axis=1)
    y_ref[...] = _layer_norm_rows(h, g_ref[...], b_ref[...])


def _exp_matrix():
    return _sel_matrix().T


def _peer_out(m_flat, x3, hi, coef, tab, gain, bias, alpha, tb):
    n = x3.shape[0]
    d = ROW_TILE * LANES
    expand = _exp_matrix()
    return pl.pallas_call(
        functools.partial(_peer_out_kernel, alpha),
        grid=(n // tb,),
        in_specs=[
            pl.BlockSpec((tb * SLOTS // 2,), lambda i: (i,), memory_space=pltpu.SMEM),
            pl.BlockSpec((tb, ROW_TILE, LANES), lambda i: (i, 0, 0)),
            pl.BlockSpec((1, SLOTS, tb), lambda i: (i, 0, 0)),
            pl.BlockSpec((tb, SLOTS), lambda i: (i, 0)),
            pl.BlockSpec(expand.shape, lambda i: (0, 0)),
            pl.BlockSpec((1, d), lambda i: (0, 0)),
            pl.BlockSpec((1, d), lambda i: (0, 0)),
            pl.BlockSpec(tab.shape, lambda i: (0, 0), pipeline_mode=pl.Buffered(1)),
        ],
        out_specs=pl.BlockSpec((tb, d), lambda i: (i, 0)),
        out_shape=jax.ShapeDtypeStruct((n, d), F32),
        scratch_shapes=[pltpu.VMEM((tb, SLOTS * PACK_ROWS), F32), pltpu.VMEM((tb * ROW_TILE, LANES), F32)],
        compiler_params=_cparams(("arbitrary",)),
        name="peer_out",
    )(m_flat, x3, hi, coef, expand, gain.reshape(1, d), bias.reshape(1, d), tab)


def _sel_matrix():
    k = jnp.arange(SLOTS * PACK_ROWS)
    slot, row = k // PACK_ROWS, k % PACK_ROWS
    col = jnp.where(row < ROW_TILE, slot, SLOTS + slot)
    return (col[:, None] == jnp.arange(2 * SLOTS)[None, :]).astype(BF16)


def _peer_act(m_flat, x3, hi, gate, tab, tb):
    n = x3.shape[0]
    sel = _sel_matrix()
    return pl.pallas_call(
        _peer_act_kernel,
        grid=(n // tb,),
        in_specs=[
            pl.BlockSpec((tb * SLOTS // 2,), lambda i: (i,), memory_space=pltpu.SMEM),
            pl.BlockSpec((tb, ROW_TILE, LANES), lambda i: (i, 0, 0)),
            pl.BlockSpec((1, SLOTS, tb), lambda i: (i, 0, 0)),
            pl.BlockSpec((1, SLOTS, tb), lambda i: (i, 0, 0)),
            pl.BlockSpec(sel.shape, lambda i: (0, 0)),
            pl.BlockSpec(tab.shape, lambda i: (0, 0), pipeline_mode=pl.Buffered(1)),
        ],
        out_specs=pl.BlockSpec((tb, SLOTS), lambda i: (i, 0)),
        out_shape=jax.ShapeDtypeStruct((n, SLOTS), F32),
        scratch_shapes=[pltpu.VMEM((tb, SLOTS * PACK_ROWS), F32)],
        compiler_params=_cparams(("arbitrary",)),
        name="peer_act",
    )(m_flat, x3, hi, gate, sel, tab)


def _peer_layer(x1, wq, sk, u_tab, v_tab, gain, bias, alpha):
    n, d = x1.shape
    m, hi, gate = _route(x1, wq, sk, min(ROUTE_TILE, n), PEER_TILE)
    m_flat = m.reshape(n * SLOTS // 2)
    x3 = x1.reshape(n, ROW_TILE, LANES)
    coef = _peer_act(m_flat, x3, hi, gate, u_tab, PEER_TILE)
    return _peer_out(m_flat, x3, hi, coef, v_tab, gain, bias, alpha, PEER_TILE)


ROUTE_TILE = 512
PEER_TILE = 128
PROJ_TILE = 256
MIX_TILE = 256


def kernel(x_prompt, x_sample, cache_k_win, cache_v_win, state_conv, w_in, w_out, conv_w, conv_b, conv_ln_g,
           conv_ln_b, ln1_g, ln1_b, w_query, sub_keys, expert_u, expert_v, ln2_g, ln2_b):
    depth, d_model, in_cols = w_in.shape
    batch, seq, _ = x_prompt.shape
    dec_batch, dec_seq, _ = x_sample.shape
    conv_ch = conv_w.shape[2]
    att_w = (in_cols - 2 * conv_ch) // 3
    heads = att_w // HEAD_DIM
    lb = cache_k_win.shape[2]
    windows = tuple((WIN_STEPS * d, d) for d in DILATIONS)
    w_max = windows[-1][0]
    assert lb == w_max and seq >= w_max and d_model == ROW_TILE * LANES
    alpha = (2.0 * depth) ** 0.25

    hp = x_prompt.reshape(batch * seq, d_model)
    hs = x_sample.reshape(dec_batch * dec_seq, d_model)
    outs = [[] for _ in range(6)]
    for l in range(depth):
        w_in_l, w_out_l = w_in[l].astype(BF16), w_out[l].astype(BF16)
        wq_l, sk_l = w_query[l].astype(BF16), sub_keys[l].astype(BF16)
        u_tab, v_tab = _pack_table(expert_u[l]), _pack_table(expert_v[l])
        mix_args = (w_out_l, conv_w[l], conv_b[l], conv_ln_g[l], conv_ln_b[l], ln1_g[l], ln1_b[l], alpha)
        peer_args = (wq_l, sk_l, u_tab, v_tab, ln2_g[l], ln2_b[l], alpha)

        q, k, v, glu, kt, vt = _in_proj(hp, w_in_l, att_w, PROJ_TILE, seq, w_max)
        att = _attn_prompt(q, k, v, batch, seq)
        past = jnp.zeros((batch, CONV_K - 1, conv_ch), F32)
        x1 = _mix(hp, att, glu, past, *mix_args, batch, seq, MIX_TILE)
        outs[0].append(kt.reshape(batch, heads, HEAD_DIM, w_max).transpose(0, 3, 1, 2))
        outs[1].append(vt.reshape(batch, heads, HEAD_DIM, w_max).transpose(0, 3, 1, 2))
        outs[2].append(glu.reshape(batch, seq, conv_ch)[:, seq - (CONV_K - 1):])
        hp = _peer_layer(x1, *peer_args)

        q, k, v, glu = _in_proj(hs, w_in_l, att_w, dec_batch * dec_seq)
        new3 = lambda a: a.reshape(dec_batch, dec_seq, att_w)
        to_feature_major = lambda c: c.transpose(0, 2, 3, 1).reshape(dec_batch, att_w, lb)
        att, nk, nv = _attn_sample(new3(q), new3(k), new3(v), to_feature_major(cache_k_win[l]),
                                   to_feature_major(cache_v_win[l]), windows)
        x1 = _mix(hs, att.reshape(dec_batch * dec_seq, att_w), glu, state_conv[l], *mix_args,
                  dec_batch, dec_seq, dec_seq)
        conv_in = jnp.concatenate([state_conv[l], glu.reshape(dec_batch, dec_seq, conv_ch)], axis=1)
        to_position_major = lambda c: c.reshape(dec_batch, heads, HEAD_DIM, lb).transpose(0, 3, 1, 2)
        outs[3].append(to_position_major(nk))
        outs[4].append(to_position_major(nv))
        outs[5].append(conv_in[:, dec_seq:])
        hs = _peer_layer(x1, *peer_args)

    stack = lambda rows: jnp.stack(rows, 0)
    return (hp.reshape(batch, seq, d_model), hs.reshape(dec_batch, dec_seq, d_model),
            stack(outs[0]), stack(outs[1]), stack(outs[2]), stack(outs[3]), stack(outs[4]), stack(outs[5]))
```

```python
import functools
import math

import jax
import jax.numpy as jnp
from jax import lax
from jax.experimental import pallas as pl
from jax.experimental.pallas import tpu as pltpu

F32 = jnp.float32
BF16 = jnp.bfloat16

HEAD_DIM = 64
N_KEYS = 128
PEER_HEADS = 8
PEER_TOPK = 16
SLOTS = PEER_HEADS * PEER_TOPK
CONV_K = 31
DILATIONS = (1, 4, 16)
WIN_STEPS = 128
LN_EPS = 1e-5

LANES = 128
ROW_TILE = 8
PACK_ROWS = 16
CHUNK = 16
TOKEN_UNROLL = 32
ACT_LHS_COPIES = 4
OUT_LHS_COPIES = 3
VMEM_LIMIT = 56 * 1024 * 1024
PACK_TILE = 512


def _cparams(sem):
    return pltpu.CompilerParams(dimension_semantics=sem, vmem_limit_bytes=VMEM_LIMIT)


def _nt_dot(a, b):
    return lax.dot_general(a, b, (((1,), (1,)), ((), ())), preferred_element_type=F32)


def _split_bf16(x):
    hi = x.astype(BF16)
    lo = (x - hi.astype(F32)).astype(BF16)
    return hi, lo


def _dot_select(a, sel):
    hi = a.astype(BF16)
    rest = a - hi.astype(F32)
    mid = rest.astype(BF16)
    lo = (rest - mid.astype(F32)).astype(BF16)
    dot = lambda p: jnp.dot(p, sel, preferred_element_type=F32)
    return dot(hi) + dot(mid) + dot(lo)


def _in_proj_kernel(att_w, window, x_ref, w_ref, q_ref, k_ref, v_ref, glu_ref, *kv_t_refs):
    z = jnp.dot(x_ref[...].astype(BF16), w_ref[...], preferred_element_type=F32)
    q_ref[...] = z[:, :att_w]
    k_ref[...] = z[:, att_w:2 * att_w]
    v_ref[...] = z[:, 2 * att_w:3 * att_w]
    if window is not None:
        tiles_per_seq, first = window
        kt_ref, vt_ref = kv_t_refs

        @pl.when(pl.program_id(0) % tiles_per_seq >= first)
        def _():
            kt_ref[0] = z[:, att_w:2 * att_w].T
            vt_ref[0] = z[:, 2 * att_w:3 * att_w].T
    conv_ch = (z.shape[1] - 3 * att_w) // 2
    a = z[:, 3 * att_w:3 * att_w + conv_ch]
    g = z[:, 3 * att_w + conv_ch:]
    glu_ref[...] = a * (1.0 / (1.0 + jnp.exp(-g)))


def _in_proj(x, w_in, att_w, tm, seq=None, keep=None):
    n, d = x.shape
    cols = w_in.shape[1]
    conv_ch = (cols - 3 * att_w) // 2
    out = lambda w: jax.ShapeDtypeStruct((n, w), F32)
    ospec = lambda w: pl.BlockSpec((tm, w), lambda i: (i, 0))
    out_specs = [ospec(att_w), ospec(att_w), ospec(att_w), ospec(conv_ch)]
    out_shape = [out(att_w), out(att_w), out(att_w), out(conv_ch)]
    window = None
    if keep is not None:
        tiles_per_seq, first = seq // tm, (seq - keep) // tm
        window = (tiles_per_seq, first)
        wspec = pl.BlockSpec((1, att_w, tm),
                             lambda i: (i // tiles_per_seq, 0, jnp.maximum(i % tiles_per_seq - first, 0)))
        out_specs += [wspec, wspec]
        out_shape += [jax.ShapeDtypeStruct((n // seq, att_w, keep), F32)] * 2
    return pl.pallas_call(
        functools.partial(_in_proj_kernel, att_w, window),
        grid=(n // tm,),
        in_specs=[pl.BlockSpec((tm, d), lambda i: (i, 0)),
                  pl.BlockSpec((d, cols), lambda i: (0, 0))],
        out_specs=out_specs,
        out_shape=out_shape,
        compiler_params=_cparams(("arbitrary",)),
        name="in_proj",
    )(x, w_in)


Q_BLOCK = 128
ATTN_UNROLL = 8
COPY_ROWS = 256


def _attn_prompt_kernel(q_ref, k_ref, v_ref, o_ref, qs, ks, vs, qf, kf, vf, *acc_refs):
    seq = q_ref.shape[0]
    accs = [acc_refs[3 * b:3 * b + 3] for b in range(len(DILATIONS))]
    scale = HEAD_DIM ** -0.5
    lane = lax.broadcasted_iota(jnp.int32, (1, LANES), 1)
    first_head = lane < HEAD_DIM
    qi = lax.broadcasted_iota(jnp.int32, (Q_BLOCK, 2 * Q_BLOCK), 0)
    kj = lax.broadcasted_iota(jnp.int32, (Q_BLOCK, 2 * Q_BLOCK), 1)
    rel = qi - kj

    for level, (d, (o_dst, m_dst, l_dst)) in enumerate(zip(DILATIONS, accs)):
        n_sub = seq // d
        blocks_per_sub = n_sub // Q_BLOCK
        d_prev = DILATIONS[level - 1] if level else 1
        n_prev, ratio = seq // d_prev, d // d_prev
        srcs = (q_ref, k_ref, v_ref) if level < 2 else (qf, kf, vf)
        keep_f32 = 0 < level < len(DILATIONS) - 1
        for r in range(d):
            for c in range(n_sub // COPY_ROWS):
                start = (r % d_prev) * n_prev + r // d_prev + ratio * COPY_ROWS * c
                src = pl.ds(start, COPY_ROWS, stride=ratio) if ratio > 1 else pl.ds(start, COPY_ROWS)
                dst = pl.ds(r * n_sub + COPY_ROWS * c, COPY_ROWS)
                rows = [ref[src, :] for ref in srcs]
                if keep_f32:
                    for ref, x in zip((qf, kf, vf), rows):
                        ref[dst, :] = x
                qs[dst, :] = (rows[0] * scale).astype(BF16)
                ks[dst, :] = rows[1].astype(BF16)
                vs[dst, :] = rows[2].astype(BF16)

        def one_block(i):
            il = i % blocks_per_sub
            has_prev = il > 0
            row0 = pl.multiple_of(i * Q_BLOCK, Q_BLOCK)
            kstart = pl.multiple_of(jnp.where(has_prev, row0 - Q_BLOCK, row0), Q_BLOCK)
            delta = rel + jnp.where(has_prev, Q_BLOCK, 0)
            valid = (delta >= 0) & (delta <= WIN_STEPS)
            qb = qs[pl.ds(row0, Q_BLOCK), :]
            kb = ks[pl.ds(kstart, 2 * Q_BLOCK), :]
            vb = vs[pl.ds(kstart, 2 * Q_BLOCK), :]
            outs = []
            for head_mask in (first_head, jnp.logical_not(first_head)):
                qh = jnp.where(head_mask, qb, jnp.zeros_like(qb))
                s = jnp.where(valid, _nt_dot(qh, kb), -jnp.inf)
                m = jnp.max(s, axis=1, keepdims=True)
                p = jnp.exp(s - m)
                l = jnp.sum(p, axis=1, keepdims=True)
                o = jnp.dot(p.astype(BF16), vb, preferred_element_type=F32)
                outs.append((o, m, l))
            (o0, m0, l0), (o1, m1, l1) = outs
            rows = pl.ds(row0, Q_BLOCK)
            o_dst[rows, :] = jnp.where(first_head, o0, o1)
            m_dst[rows, :] = jnp.where(first_head, m0, m1)
            l_dst[rows, :] = jnp.where(first_head, l0, l1)

        def blocks(it, carry):
            for u in range(ATTN_UNROLL):
                one_block(it * ATTN_UNROLL + u)
            return carry

        lax.fori_loop(0, seq // (Q_BLOCK * ATTN_UNROLL), blocks, 0)

    for level in range(len(DILATIONS) - 1, 0, -1):
        d_hi, d_lo = DILATIONS[level], DILATIONS[level - 1]
        (o_hi, m_hi, l_hi), (o_lo, m_lo, l_lo) = accs[level], accs[level - 1]
        n_hi, n_lo, ratio = seq // d_hi, seq // d_lo, d_hi // d_lo
        for r in range(d_hi):
            for c in range(n_hi // COPY_ROWS):
                lo = pl.ds((r % d_lo) * n_lo + r // d_lo + ratio * COPY_ROWS * c, COPY_ROWS, stride=ratio)
                hi = pl.ds(r * n_hi + COPY_ROWS * c, COPY_ROWS)
                m1, m2 = m_lo[lo, :], m_hi[hi, :]
                m = jnp.maximum(m1, m2)
                a1, a2 = jnp.exp(m1 - m), jnp.exp(m2 - m)
                o_lo[lo, :] = o_lo[lo, :] * a1 + o_hi[hi, :] * a2
                l_lo[lo, :] = l_lo[lo, :] * a1 + l_hi[hi, :] * a2
                m_lo[lo, :] = m

    on, _, ln = accs[0]
    for c in range(seq // COPY_ROWS):
        rows = pl.ds(COPY_ROWS * c, COPY_ROWS)
        o_ref[rows, :] = on[rows, :] / ln[rows, :]


def _attn_prompt(q, k, v, batch, seq):
    width = q.shape[1]
    spec = pl.BlockSpec((seq, LANES), lambda b, g: (b, g))
    f32_scr = pltpu.VMEM((seq, LANES), F32)
    bf_scr = pltpu.VMEM((seq, LANES), BF16)
    return pl.pallas_call(
        _attn_prompt_kernel,
        grid=(batch, width // LANES),
        in_specs=[spec, spec, spec],
        out_specs=spec,
        out_shape=jax.ShapeDtypeStruct(q.shape, F32),
        scratch_shapes=[bf_scr] * 3 + [f32_scr] * (3 + 3 * len(DILATIONS)),
        compiler_params=_cparams(("arbitrary", "arbitrary")),
        name="attn_prompt",
    )(q, k, v)


KEY_PAD = 128
ROLL_ROWS = 64


def _attn_sample_kernel(windows, q_ref, k_ref, v_ref, ck_ref, cv_ref, o_ref, nk_ref, nv_ref):
    t_new, width = q_ref.shape[1], q_ref.shape[2]
    lb = ck_ref.shape[2]
    heads = width // HEAD_DIM
    rows = heads * t_new
    scale = HEAD_DIM ** -0.5
    k_new, v_new = k_ref[0], v_ref[0]
    pad = jnp.zeros((KEY_PAD - t_new, width), F32)

    lane = lax.broadcasted_iota(jnp.int32, (ROLL_ROWS, KEY_PAD), 1)
    for src, new, dst in ((ck_ref, k_new, nk_ref), (cv_ref, v_new, nv_ref)):
        tail = jnp.concatenate([pad, new], axis=0).T
        for c in range(width // ROLL_ROWS):
            rs = slice(ROLL_ROWS * c, ROLL_ROWS * (c + 1))
            y = pltpu.roll(src[0, rs, :], lb - t_new, 1)
            dst[0, rs, :lb - KEY_PAD] = y[:, :lb - KEY_PAD]
            dst[0, rs, lb - KEY_PAD:] = jnp.where(lane >= KEY_PAD - t_new, tail[rs, :], y[:, lb - KEY_PAD:])

    q_rep = jnp.concatenate([q_ref[0] * scale] * heads, axis=0)
    r_head = lax.broadcasted_iota(jnp.int32, (rows, width), 0) // t_new
    l_head = lax.broadcasted_iota(jnp.int32, (rows, width), 1) // HEAD_DIM
    own = r_head == l_head
    qbd = jnp.where(own, q_rep, 0.0).astype(BF16)
    k_pad = jnp.concatenate([k_new, pad], axis=0).astype(BF16)
    v_pad = jnp.concatenate([v_new, pad], axis=0).astype(BF16)
    s_cache = jnp.dot(qbd, ck_ref[0].astype(BF16), preferred_element_type=F32)
    s = jnp.concatenate([s_cache, _nt_dot(qbd, k_pad)], axis=1)
    tok = lax.broadcasted_iota(jnp.int32, s.shape, 0) % t_new
    key = lax.broadcasted_iota(jnp.int32, s.shape, 1)
    dist = lb + tok - key
    cnt = jnp.zeros(s.shape, F32)
    for w, d in windows:
        cnt = cnt + ((dist >= 0) & (dist <= w) & (dist % d == 0)).astype(F32)
    s = jnp.where(cnt > 0, s, -jnp.inf)
    m = jnp.max(s, axis=1, keepdims=True)
    p = (cnt * jnp.exp(s - m)).astype(BF16)
    l = jnp.sum(p.astype(F32), axis=1, keepdims=True)
    o = _nt_dot(p[:, :lb], cv_ref[0].astype(BF16)) + jnp.dot(p[:, lb:], v_pad, preferred_element_type=F32)
    o = jnp.where(own, o / l, 0.0).reshape(heads, t_new, width)
    o_ref[0] = jnp.sum(o, axis=0)


def _attn_sample(q, k, v, cache_kt, cache_vt, windows):
    batch, t_new, width = q.shape
    lb = cache_kt.shape[2]
    new_spec = pl.BlockSpec((1, t_new, width), lambda b: (b, 0, 0))
    cache_spec = pl.BlockSpec((1, width, lb), lambda b: (b, 0, 0))
    return pl.pallas_call(
        functools.partial(_attn_sample_kernel, windows),
        grid=(batch,),
        in_specs=[new_spec, new_spec, new_spec, cache_spec, cache_spec],
        out_specs=[new_spec, cache_spec, cache_spec],
        out_shape=[jax.ShapeDtypeStruct(q.shape, F32), jax.ShapeDtypeStruct(cache_kt.shape, F32),
                   jax.ShapeDtypeStruct(cache_vt.shape, F32)],
        compiler_params=_cparams(("arbitrary",)),
        name="attn_sample",
    )(q, k, v, cache_kt, cache_vt)


HALO = 32


def _layer_norm_rows(h, gain, bias):
    mu = jnp.mean(h, axis=-1, keepdims=True)
    hc = h - mu
    var = jnp.mean(hc * hc, axis=-1, keepdims=True)
    return hc * lax.rsqrt(var + LN_EPS) * gain + bias


def _mix_kernel(alpha, chunk, x_ref, att_ref, glu_ref, halo_ref, past_ref, wo_ref, cw_ref, cb_ref, cg_ref,
                cbeta_ref, g1_ref, b1_ref, x1_ref, xp, conv_scr, shifted):
    tm, conv_ch = glu_ref.shape
    first_tile = pl.program_id(1) == 0
    xp[0:HALO, :] = jnp.where(first_tile, past_ref[0], halo_ref[...])
    xp[HALO:HALO + tm, :] = glu_ref[...]
    lead = HALO - (CONV_K - 1)
    span = shifted.shape[1]
    for s in range(1, ROW_TILE):
        for r0 in range(0, span, COPY_ROWS):
            r1 = min(r0 + COPY_ROWS, span)
            shifted[s, r0:r1, :] = xp[r0 + s:r1 + s, :]
    for rc in range(tm // chunk):
        acc = jnp.zeros((chunk, conv_ch), F32)
        for tap in range(CONV_K):
            s, base = (lead + tap) % ROW_TILE, rc * chunk + (lead + tap) // ROW_TILE * ROW_TILE
            rows = xp[base:base + chunk, :] if s == 0 else shifted[s, base:base + chunk, :]
            acc = acc + cw_ref[tap:tap + 1, :] * rows
        y = _layer_norm_rows(acc + cb_ref[...], cg_ref[...], cbeta_ref[...])
        conv_scr[rc * chunk:(rc + 1) * chunk, :] = y * (1.0 / (1.0 + jnp.exp(-y)))
    att_w = att_ref.shape[1]
    mix = jnp.dot(att_ref[...].astype(BF16), wo_ref[0:att_w, :], preferred_element_type=F32)
    mix = mix + jnp.dot(conv_scr[...].astype(BF16), wo_ref[att_w:, :], preferred_element_type=F32)
    x1_ref[...] = _layer_norm_rows(alpha * x_ref[...] + mix, g1_ref[...], b1_ref[...])


def _mix(x, att, glu, past, w_out, conv_w, conv_b, conv_g, conv_beta, g1, b1, alpha, batch, seq, tm):
    n, d = x.shape
    att_w, conv_ch = att.shape[1], glu.shape[1]
    tiles = seq // tm
    past = jnp.pad(past, ((0, 0), (HALO - (CONV_K - 1), 0), (0, 0)))
    row = lambda w: pl.BlockSpec((tm, w), lambda b, i: (b * tiles + i, 0))
    vec = lambda w: pl.BlockSpec((1, w), lambda b, i: (0, 0))
    halo_blocks = n // HALO
    halo = pl.BlockSpec((HALO, conv_ch),
                        lambda b, i: (jnp.clip((b * seq + i * tm) // HALO - 1, 0, halo_blocks - 1), 0))
    return pl.pallas_call(
        functools.partial(_mix_kernel, alpha, min(tm, 128)),
        grid=(batch, tiles),
        in_specs=[row(d), row(att_w), row(conv_ch), halo,
                  pl.BlockSpec((1, HALO, conv_ch), lambda b, i: (b, 0, 0)),
                  pl.BlockSpec(w_out.shape, lambda b, i: (0, 0)),
                  pl.BlockSpec(conv_w.shape, lambda b, i: (0, 0)),
                  vec(conv_ch), vec(conv_ch), vec(conv_ch), vec(d), vec(d)],
        out_specs=row(d),
        out_shape=jax.ShapeDtypeStruct((n, d), F32),
        scratch_shapes=[pltpu.VMEM((HALO + tm, conv_ch), F32), pltpu.VMEM((tm, conv_ch), F32),
                        pltpu.VMEM((ROW_TILE, HALO + tm - ROW_TILE, conv_ch), F32)],
        compiler_params=_cparams(("arbitrary", "arbitrary")),
        name="mix",
    )(x, att, glu, glu, past, w_out, conv_w, conv_b.reshape(1, -1), conv_g.reshape(1, -1),
      conv_beta.reshape(1, -1), g1.reshape(1, -1), b1.reshape(1, -1))


def _scan_max(s_ref, rows_ref, prev):
    rows, tt = s_ref.shape
    prev = jnp.broadcast_to(prev, (ROW_TILE, tt))
    m = jnp.full((ROW_TILE, tt), -jnp.inf, F32)
    g = jnp.zeros((ROW_TILE, tt), F32)
    for j in range(rows // ROW_TILE):
        rs = slice(j * ROW_TILE, (j + 1) * ROW_TILE)
        sj = jnp.where(rows_ref[rs, :] == prev, -jnp.inf, s_ref[rs, :])
        s_ref[rs, :] = sj
        gt = sj > m
        m = jnp.where(gt, sj, m)
        g = jnp.where(gt, float(j), g)
    key = g * float(ROW_TILE) + rows_ref[0:ROW_TILE, :]
    top = jnp.max(m, axis=0, keepdims=True)
    idx = jnp.min(jnp.where(m == top, key, float(rows)), axis=0, keepdims=True)
    return top, idx


def _route_kernel(half_experts, x_ref, wq_ref, sk_ref, m_ref, hi_ref, gate_ref, v0, i0, v1, i1, bs, be, cs, ce,
                  sa, sb, key_rows):
    tt = x_ref.shape[0]
    q = jnp.dot(x_ref[...].astype(BF16), wq_ref[...], preferred_element_type=F32)
    qb = q.astype(BF16)
    sa[...] = _nt_dot(sk_ref[0, 0], qb[:, :LANES])
    sb[...] = _nt_dot(sk_ref[0, 1], qb[:, LANES:])
    key_rows[...] = lax.broadcasted_iota(jnp.int32, (N_KEYS, tt), 0).astype(F32)

    def first(r, prev):
        ma, ia = _scan_max(sa, key_rows, prev[0])
        mb, ib = _scan_max(sb, key_rows, prev[1])
        v0[pl.ds(r, 1), :] = ma
        i0[pl.ds(r, 1), :] = ia
        v1[pl.ds(r, 1), :] = mb
        i1[pl.ds(r, 1), :] = ib
        return ia, ib

    none = jnp.full((1, tt), -1.0, F32)
    lax.fori_loop(0, PEER_TOPK, first, (none, none))

    off = 0
    for a in range(PEER_TOPK):
        nb = PEER_TOPK // (a + 1)
        cs[off:off + nb, :] = v0[a:a + 1, :] + v1[0:nb, :]
        ce[off:off + nb, :] = i0[a:a + 1, :] * float(N_KEYS) + i1[0:nb, :]
        off += nb
    cs[off:, :] = jnp.full((cs.shape[0] - off, tt), -jnp.inf, F32)
    ce[off:, :] = jnp.full((cs.shape[0] - off, tt), -1.0, F32)
    pairs = PEER_TOPK // 2
    cand_rows = cs.shape[0]

    def second(r, prev):
        m, idx = _scan_max(cs, key_rows, prev)
        row = lax.shift_right_logical(r, 1) + pairs * (r & 1)
        bs[pl.ds(row, 1), :] = m
        picked = jnp.where(key_rows[0:cand_rows, :] == idx, ce[...], -1.0)
        be[pl.ds(row, 1), :] = jnp.max(picked, axis=0, keepdims=True)
        return idx

    lax.fori_loop(0, PEER_TOPK, second, none)

    best = bs[...]
    p = jnp.exp(best - jnp.max(best, axis=0, keepdims=True))
    gate = p / jnp.sum(p, axis=0, keepdims=True)
    e = be[...].astype(jnp.int32)
    high = (e >= half_experts).astype(jnp.int32)
    rows = (e - high * half_experts) * ROW_TILE
    words = rows[:pairs] | (rows[pairs:] << 16)
    tb = m_ref.shape[2]
    for k in range(tt // tb):
        lanes = slice(k * tb, (k + 1) * tb)
        gate_ref[k] = gate[:, lanes]
        hi_ref[k] = high[:, lanes]
        m_ref[k] = words[:, lanes]


def _route(x, wq, sk, tt, tb):
    n, d = x.shape
    half_experts = N_KEYS * N_KEYS // 2
    cand_rows = -(-sum(PEER_TOPK // (a + 1) for a in range(PEER_TOPK)) // ROW_TILE) * ROW_TILE
    out = jax.ShapeDtypeStruct((n // tb, SLOTS, tb), jnp.int32)
    ospec = pl.BlockSpec((tt // tb, PEER_TOPK, tb), lambda i, h: (i, h, 0))
    return pl.pallas_call(
        functools.partial(_route_kernel, half_experts),
        grid=(n // tt, PEER_HEADS),
        in_specs=[
            pl.BlockSpec((tt, d), lambda i, h: (i, 0)),
            pl.BlockSpec((d, 2 * LANES), lambda i, h: (0, h)),
            pl.BlockSpec((1, 2, N_KEYS, LANES), lambda i, h: (h, 0, 0, 0)),
        ],
        out_specs=[pl.BlockSpec((tt // tb, PEER_TOPK // 2, tb), lambda i, h: (i, h, 0)), ospec, ospec],
        out_shape=[jax.ShapeDtypeStruct((n // tb, SLOTS // 2, tb), jnp.int32), out,
                   jax.ShapeDtypeStruct(out.shape, F32)],
        scratch_shapes=([pltpu.VMEM((PEER_TOPK, tt), F32)] * 6 + [pltpu.VMEM((cand_rows, tt), F32)] * 2
                        + [pltpu.VMEM((N_KEYS, tt), F32)] * 3),
        compiler_params=_cparams(("arbitrary", "arbitrary")),
        name="peer_route",
    )(x, wq, sk)


def _bf16_bits(x):
    return pltpu.bitcast(x.astype(BF16).astype(F32), jnp.uint32)


def _pack_kernel(lo_ref, hi_ref, o_ref):
    half_rows = ROW_TILE // 2
    for i in range(lo_ref.shape[0] // ROW_TILE):
        src = slice(i * ROW_TILE, (i + 1) * ROW_TILE)
        for part, ref in enumerate((lo_ref, hi_ref)):
            for q in range(half_rows):
                a = _bf16_bits(ref[src, 2 * q * LANES:(2 * q + 1) * LANES])
                b = _bf16_bits(ref[src, (2 * q + 1) * LANES:(2 * q + 2) * LANES])
                word = b | lax.shift_right_logical(a, jnp.uint32(16))
                rows = pl.ds(i * ROW_TILE * ROW_TILE + part * half_rows + q, ROW_TILE, stride=ROW_TILE)
                o_ref[rows, :] = pltpu.bitcast(word, jnp.int32)


def _pack_table(tab, rows=PACK_TILE):
    n, d = tab.shape
    half = n // 2
    steps = half // rows
    return pl.pallas_call(
        _pack_kernel,
        grid=(steps,),
        in_specs=[pl.BlockSpec((rows, d), lambda i: (i, 0)),
                  pl.BlockSpec((rows, d), lambda i: (i + steps, 0))],
        out_specs=pl.BlockSpec((rows * ROW_TILE, LANES), lambda i: (i, 0)),
        out_shape=jax.ShapeDtypeStruct((half * ROW_TILE, LANES), jnp.int32),
        compiler_params=_cparams(("arbitrary",)),
        name="pack_table",
    )(tab, tab)


def _load_tile(tab_ref, row):
    return pltpu.bitcast(tab_ref[pl.ds(pl.multiple_of(row, ROW_TILE), ROW_TILE), :], BF16)


def _load_chunk(tab_ref, m_ref, c, t, tb):
    words = [m_ref[(c * (CHUNK // 2) + j) * tb + t] for j in range(CHUNK // 2)]
    return ([_load_tile(tab_ref, w & 0xFFFF) for w in words]
            + [_load_tile(tab_ref, lax.shift_right_logical(w, 16)) for w in words])


def _diag_mask(rows):
    r = lax.broadcasted_iota(jnp.int32, (rows, CHUNK * PACK_ROWS), 0)
    c = lax.broadcasted_iota(jnp.int32, (rows, CHUNK * PACK_ROWS), 1)
    return (r % PACK_ROWS) == (c % PACK_ROWS)


def _peer_act_kernel(m_ref, x_ref, hi_ref, gate_ref, sel_ref, tab_ref, coef_ref, r_scr):
    tb = x_ref.shape[0]
    diag = _diag_mask(2 * PACK_ROWS)

    def tokens(it, carry):
        for u in range(TOKEN_UNROLL):
            t = it * TOKEN_UNROLL + u
            xh, xl = _split_bf16(x_ref[t])
            lhs = jnp.concatenate([xh, xh, xl, xl] * ACT_LHS_COPIES, axis=0)
            for c in range(SLOTS // CHUNK):
                tiles = _load_chunk(tab_ref, m_ref, c, t, tb)
                g = jnp.concatenate(tiles, axis=0)
                o = jnp.where(diag, _nt_dot(lhs, g)[:2 * PACK_ROWS], 0.0)
                r_scr[pl.ds(t, 1), c * 256:(c + 1) * 256] = jnp.sum(o, axis=0, keepdims=True)
        return carry

    lax.fori_loop(0, tb // TOKEN_UNROLL, tokens, 0)
    act2 = _dot_select(r_scr[...], sel_ref[...])
    high = hi_ref[0].astype(F32).T > 0
    act = jnp.where(high, act2[:, SLOTS:], act2[:, :SLOTS])
    gelu = 0.5 * act * (1.0 + lax.erf(act * (1.0 / math.sqrt(2.0))))
    coef_ref[...] = gate_ref[0].T * gelu


def _peer_out_kernel(alpha, m_ref, x_ref, hi_ref, coef_ref, exp_ref, g_ref, b_ref, tab_ref, y_ref, ce_scr, h_scr):
    tb = x_ref.shape[0]
    diag = _diag_mask(PACK_ROWS)
    coef = coef_ref[...]
    high = hi_ref[0].astype(F32).T > 0
    c2 = jnp.concatenate([jnp.where(high, 0.0, coef), jnp.where(high, coef, 0.0)], axis=1)
    ce_scr[...] = _dot_select(c2, exp_ref[...])
    def tokens(it, carry):
        for u in range(TOKEN_UNROLL):
            t = it * TOKEN_UNROLL + u
            acc = jnp.zeros((PACK_ROWS, LANES), F32)
            for c in range(SLOTS // CHUNK):
                tiles = _load_chunk(tab_ref, m_ref, c, t, tb)
                g = jnp.concatenate(tiles, axis=0)
                ce = ce_scr[pl.ds(t, 1), c * 256:(c + 1) * 256]
                cm = jnp.where(diag, jnp.broadcast_to(ce, (PACK_ROWS, 256)), 0.0)
                ch, cl = _split_bf16(cm)
                lhs = jnp.concatenate([ch, cl] * OUT_LHS_COPIES, axis=0)
                o = jnp.dot(lhs, g, preferred_element_type=F32)[:2 * PACK_ROWS]
                acc = acc + o[:PACK_ROWS] + o[PACK_ROWS:]
            rows = pl.ds(pl.multiple_of(t * ROW_TILE, ROW_TILE), ROW_TILE)
            h_scr[rows, :] = alpha * x_ref[t] + acc[:ROW_TILE] + acc[ROW_TILE:]
        return carry

    lax.fori_loop(0, tb // TOKEN_UNROLL, tokens, 0)

    h = jnp.concatenate([h_scr[pl.ds(j, tb, stride=ROW_TILE), :] for j in range(ROW_TILE)], axis=1)
    y_ref[...] = _layer_norm_rows(h, g_ref[...], b_ref[...])


def _exp_matrix():
    return _sel_matrix().T


def _peer_out(m_flat, x3, hi, coef, tab, gain, bias, alpha, tb):
    n = x3.shape[0]
    d = ROW_TILE * LANES
    expand = _exp_matrix()
    return pl.pallas_call(
        functools.partial(_peer_out_kernel, alpha),
        grid=(n // tb,),
        in_specs=[
            pl.BlockSpec((tb * SLOTS // 2,), lambda i: (i,), memory_space=pltpu.SMEM),
            pl.BlockSpec((tb, ROW_TILE, LANES), lambda i: (i, 0, 0)),
            pl.BlockSpec((1, SLOTS, tb), lambda i: (i, 0, 0)),
            pl.BlockSpec((tb, SLOTS), lambda i: (i, 0)),
            pl.BlockSpec(expand.shape, lambda i: (0, 0)),
            pl.BlockSpec((1, d), lambda i: (0, 0)),
            pl.BlockSpec((1, d), lambda i: (0, 0)),
            pl.BlockSpec(tab.shape, lambda i: (0, 0), pipeline_mode=pl.Buffered(1)),
        ],
        out_specs=pl.BlockSpec((tb, d), lambda i: (i, 0)),
        out_shape=jax.ShapeDtypeStruct((n, d), F32),
        scratch_shapes=[pltpu.VMEM((tb, SLOTS * PACK_ROWS), F32), pltpu.VMEM((tb * ROW_TILE, LANES), F32)],
        compiler_params=_cparams(("arbitrary",)),
        name="peer_out",
    )(m_flat, x3, hi, coef, expand, gain.reshape(1, d), bias.reshape(1, d), tab)


def _sel_matrix():
    k = jnp.arange(SLOTS * PACK_ROWS)
    slot, row = k // PACK_ROWS, k % PACK_ROWS
    col = jnp.where(row < ROW_TILE, slot, SLOTS + slot)
    return (col[:, None] == jnp.arange(2 * SLOTS)[None, :]).astype(BF16)


def _peer_act(m_flat, x3, hi, gate, tab, tb):
    n = x3.shape[0]
    sel = _sel_matrix()
    return pl.pallas_call(
        _peer_act_kernel,
        grid=(n // tb,),
        in_specs=[
            pl.BlockSpec((tb * SLOTS // 2,), lambda i: (i,), memory_space=pltpu.SMEM),
            pl.BlockSpec((tb, ROW_TILE, LANES), lambda i: (i, 0, 0)),
            pl.BlockSpec((1, SLOTS, tb), lambda i: (i, 0, 0)),
            pl.BlockSpec((1, SLOTS, tb), lambda i: (i, 0, 0)),
            pl.BlockSpec(sel.shape, lambda i: (0, 0)),
            pl.BlockSpec(tab.shape, lambda i: (0, 0), pipeline_mode=pl.Buffered(1)),
        ],
        out_specs=pl.BlockSpec((tb, SLOTS), lambda i: (i, 0)),
        out_shape=jax.ShapeDtypeStruct((n, SLOTS), F32),
        scratch_shapes=[pltpu.VMEM((tb, SLOTS * PACK_ROWS), F32)],
        compiler_params=_cparams(("arbitrary",)),
        name="peer_act",
    )(m_flat, x3, hi, gate, sel, tab)


def _peer_layer(x1, wq, sk, u_tab, v_tab, gain, bias, alpha):
    n = x1.shape[0]
    m, hi, gate = _route(x1, wq, sk, min(ROUTE_TILE, n), PEER_TILE)
    m_flat = m.reshape(n * SLOTS // 2)
    x3 = x1.reshape(n, ROW_TILE, LANES)
    coef = _peer_act(m_flat, x3, hi, gate, u_tab, PEER_TILE)
    return _peer_out(m_flat, x3, hi, coef, v_tab, gain, bias, alpha, PEER_TILE)


ROUTE_TILE = 512
PEER_TILE = 128
PROJ_TILE = 256
MIX_TILE = 256


def kernel(x_prompt, x_sample, cache_k_win, cache_v_win, state_conv, w_in, w_out, conv_w, conv_b, conv_ln_g,
           conv_ln_b, ln1_g, ln1_b, w_query, sub_keys, expert_u, expert_v, ln2_g, ln2_b):
    depth, d_model, in_cols = w_in.shape
    batch, seq, _ = x_prompt.shape
    dec_batch, dec_seq, _ = x_sample.shape
    conv_ch = conv_w.shape[2]
    att_w = (in_cols - 2 * conv_ch) // 3
    heads = att_w // HEAD_DIM
    lb = cache_k_win.shape[2]
    windows = tuple((WIN_STEPS * d, d) for d in DILATIONS)
    w_max = windows[-1][0]
    assert lb == w_max and seq >= w_max and d_model == ROW_TILE * LANES
    alpha = (2.0 * depth) ** 0.25

    hp = x_prompt.reshape(batch * seq, d_model)
    hs = x_sample.reshape(dec_batch * dec_seq, d_model)
    outs = [[] for _ in range(6)]
    for l in range(depth):
        w_in_l, w_out_l = w_in[l].astype(BF16), w_out[l].astype(BF16)
        wq_l, sk_l = w_query[l].astype(BF16), sub_keys[l].astype(BF16)
        u_tab, v_tab = _pack_table(expert_u[l]), _pack_table(expert_v[l])
        mix_args = (w_out_l, conv_w[l], conv_b[l], conv_ln_g[l], conv_ln_b[l], ln1_g[l], ln1_b[l], alpha)
        peer_args = (wq_l, sk_l, u_tab, v_tab, ln2_g[l], ln2_b[l], alpha)

        q, k, v, glu, kt, vt = _in_proj(hp, w_in_l, att_w, PROJ_TILE, seq, w_max)
        att = _attn_prompt(q, k, v, batch, seq)
        past = jnp.zeros((batch, CONV_K - 1, conv_ch), F32)
        x1 = _mix(hp, att, glu, past, *mix_args, batch, seq, MIX_TILE)
        outs[0].append(kt.reshape(batch, heads, HEAD_DIM, w_max).transpose(0, 3, 1, 2))
        outs[1].append(vt.reshape(batch, heads, HEAD_DIM, w_max).transpose(0, 3, 1, 2))
        outs[2].append(glu.reshape(batch, seq, conv_ch)[:, seq - (CONV_K - 1):])
        hp = _peer_layer(x1, *peer_args)

        q, k, v, glu = _in_proj(hs, w_in_l, att_w, dec_batch * dec_seq)
        new3 = lambda a: a.reshape(dec_batch, dec_seq, att_w)
        to_feature_major = lambda c: c.transpose(0, 2, 3, 1).reshape(dec_batch, att_w, lb)
        att, nk, nv = _attn_sample(new3(q), new3(k), new3(v), to_feature_major(cache_k_win[l]),
                                   to_feature_major(cache_v_win[l]), windows)
        x1 = _mix(hs, att.reshape(dec_batch * dec_seq, att_w), glu, state_conv[l], *mix_args,
                  dec_batch, dec_seq, dec_seq)
        conv_in = jnp.concatenate([state_conv[l], glu.reshape(dec_batch, dec_seq, conv_ch)], axis=1)
        to_position_major = lambda c: c.reshape(dec_batch, heads, HEAD_DIM, lb).transpose(0, 3, 1, 2)
        outs[3].append(to_position_major(nk))
        outs[4].append(to_position_major(nv))
        outs[5].append(conv_in[:, dec_seq:])
        hs = _peer_layer(x1, *peer_args)

    stack = lambda rows: jnp.stack(rows, 0)
    return (hp.reshape(batch, seq, d_model), hs.reshape(dec_batch, dec_seq, d_model),
            stack(outs[0]), stack(outs[1]), stack(outs[2]), stack(outs[3]), stack(outs[4]), stack(outs[5]))
```

```python
import functools
import math

import jax
import jax.numpy as jnp
from jax import lax
from jax.experimental import pallas as pl
from jax.experimental.pallas import tpu as pltpu

F32 = jnp.float32
BF16 = jnp.bfloat16

HEAD_DIM = 64
N_KEYS = 128
PEER_HEADS = 8
PEER_TOPK = 16
SLOTS = PEER_HEADS * PEER_TOPK
CONV_K = 31
DILATIONS = (1, 4, 16)
WIN_STEPS = 128
LN_EPS = 1e-5

LANES = 128
ROW_TILE = 8
PACK_ROWS = 16
CHUNK = 16
TOKEN_UNROLL = 32
ACT_LHS_COPIES = 4
OUT_LHS_COPIES = 3
VMEM_LIMIT = 56 * 1024 * 1024
PACK_TILE = 512


def _cparams(sem):
    return pltpu.CompilerParams(dimension_semantics=sem, vmem_limit_bytes=VMEM_LIMIT)


def _nt_dot(a, b):
    return lax.dot_general(a, b, (((1,), (1,)), ((), ())), preferred_element_type=F32)


def _split_bf16(x):
    hi = x.astype(BF16)
    lo = (x - hi.astype(F32)).astype(BF16)
    return hi, lo


def _dot_select(a, sel):
    hi = a.astype(BF16)
    rest = a - hi.astype(F32)
    mid = rest.astype(BF16)
    lo = (rest - mid.astype(F32)).astype(BF16)
    dot = lambda p: jnp.dot(p, sel, preferred_element_type=F32)
    return dot(hi) + dot(mid) + dot(lo)


def _in_proj_kernel(att_w, window, x_ref, w_ref, q_ref, k_ref, v_ref, glu_ref, *kv_t_refs):
    z = jnp.dot(x_ref[...].astype(BF16), w_ref[...], preferred_element_type=F32)
    q_ref[...] = z[:, :att_w]
    k_ref[...] = z[:, att_w:2 * att_w]
    v_ref[...] = z[:, 2 * att_w:3 * att_w]
    if window is not None:
        tiles_per_seq, first = window
        kt_ref, vt_ref = kv_t_refs

        @pl.when(pl.program_id(0) % tiles_per_seq >= first)
        def _():
            kt_ref[0] = z[:, att_w:2 * att_w].T
            vt_ref[0] = z[:, 2 * att_w:3 * att_w].T
    conv_ch = (z.shape[1] - 3 * att_w) // 2
    a = z[:, 3 * att_w:3 * att_w + conv_ch]
    g = z[:, 3 * att_w + conv_ch:]
    glu_ref[...] = a * (1.0 / (1.0 + jnp.exp(-g)))


def _in_proj(x, w_in, att_w, tm, seq=None, keep=None):
    n, d = x.shape
    cols = w_in.shape[1]
    conv_ch = (cols - 3 * att_w) // 2
    out = lambda w: jax.ShapeDtypeStruct((n, w), F32)
    ospec = lambda w: pl.BlockSpec((tm, w), lambda i: (i, 0))
    out_specs = [ospec(att_w), ospec(att_w), ospec(att_w), ospec(conv_ch)]
    out_shape = [out(att_w), out(att_w), out(att_w), out(conv_ch)]
    window = None
    if keep is not None:
        tiles_per_seq, first = seq // tm, (seq - keep) // tm
        window = (tiles_per_seq, first)
        wspec = pl.BlockSpec((1, att_w, tm),
                             lambda i: (i // tiles_per_seq, 0, jnp.maximum(i % tiles_per_seq - first, 0)))
        out_specs += [wspec, wspec]
        out_shape += [jax.ShapeDtypeStruct((n // seq, att_w, keep), F32)] * 2
    return pl.pallas_call(
        functools.partial(_in_proj_kernel, att_w, window),
        grid=(n // tm,),
        in_specs=[pl.BlockSpec((tm, d), lambda i: (i, 0)),
                  pl.BlockSpec((d, cols), lambda i: (0, 0))],
        out_specs=out_specs,
        out_shape=out_shape,
        compiler_params=_cparams(("arbitrary",)),
        name="in_proj",
    )(x, w_in)


Q_BLOCK = 128
ATTN_UNROLL = 8
COPY_ROWS = 256


def _attn_prompt_kernel(q_ref, k_ref, v_ref, o_ref, qs, ks, vs, qf, kf, vf, *acc_refs):
    seq = q_ref.shape[0]
    accs = [acc_refs[3 * b:3 * b + 3] for b in range(len(DILATIONS))]
    scale = HEAD_DIM ** -0.5
    lane = lax.broadcasted_iota(jnp.int32, (1, LANES), 1)
    first_head = lane < HEAD_DIM
    qi = lax.broadcasted_iota(jnp.int32, (Q_BLOCK, 2 * Q_BLOCK), 0)
    kj = lax.broadcasted_iota(jnp.int32, (Q_BLOCK, 2 * Q_BLOCK), 1)
    rel = qi - kj

    for level, (d, (o_dst, m_dst, l_dst)) in enumerate(zip(DILATIONS, accs)):
        n_sub = seq // d
        blocks_per_sub = n_sub // Q_BLOCK
        d_prev = DILATIONS[level - 1] if level else 1
        n_prev, ratio = seq // d_prev, d // d_prev
        srcs = (q_ref, k_ref, v_ref) if level < 2 else (qf, kf, vf)
        keep_f32 = 0 < level < len(DILATIONS) - 1
        for r in range(d):
            for c in range(n_sub // COPY_ROWS):
                start = (r % d_prev) * n_prev + r // d_prev + ratio * COPY_ROWS * c
                src = pl.ds(start, COPY_ROWS, stride=ratio) if ratio > 1 else pl.ds(start, COPY_ROWS)
                dst = pl.ds(r * n_sub + COPY_ROWS * c, COPY_ROWS)
                rows = [ref[src, :] for ref in srcs]
                if keep_f32:
                    for ref, x in zip((qf, kf, vf), rows):
                        ref[dst, :] = x
                qs[dst, :] = (rows[0] * scale).astype(BF16)
                ks[dst, :] = rows[1].astype(BF16)
                vs[dst, :] = rows[2].astype(BF16)

        def one_block(i):
            il = i % blocks_per_sub
            has_prev = il > 0
            row0 = pl.multiple_of(i * Q_BLOCK, Q_BLOCK)
            kstart = pl.multiple_of(jnp.where(has_prev, row0 - Q_BLOCK, row0), Q_BLOCK)
            delta = rel + jnp.where(has_prev, Q_BLOCK, 0)
            valid = (delta >= 0) & (delta <= WIN_STEPS)
            qb = qs[pl.ds(row0, Q_BLOCK), :]
            kb = ks[pl.ds(kstart, 2 * Q_BLOCK), :]
            vb = vs[pl.ds(kstart, 2 * Q_BLOCK), :]
            outs = []
            for head_mask in (first_head, jnp.logical_not(first_head)):
                qh = jnp.where(head_mask, qb, jnp.zeros_like(qb))
                s = jnp.where(valid, _nt_dot(qh, kb), -jnp.inf)
                m = jnp.max(s, axis=1, keepdims=True)
                p = jnp.exp(s - m)
                l = jnp.sum(p, axis=1, keepdims=True)
                o = jnp.dot(p.astype(BF16), vb, preferred_element_type=F32)
                outs.append((o, m, l))
            (o0, m0, l0), (o1, m1, l1) = outs
            rows = pl.ds(row0, Q_BLOCK)
            o_dst[rows, :] = jnp.where(first_head, o0, o1)
            m_dst[rows, :] = jnp.where(first_head, m0, m1)
            l_dst[rows, :] = jnp.where(first_head, l0, l1)

        def blocks(it, carry):
            for u in range(ATTN_UNROLL):
                one_block(it * ATTN_UNROLL + u)
            return carry

        lax.fori_loop(0, seq // (Q_BLOCK * ATTN_UNROLL), blocks, 0)

    for level in range(len(DILATIONS) - 1, 0, -1):
        d_hi, d_lo = DILATIONS[level], DILATIONS[level - 1]
        (o_hi, m_hi, l_hi), (o_lo, m_lo, l_lo) = accs[level], accs[level - 1]
        n_hi, n_lo, ratio = seq // d_hi, seq // d_lo, d_hi // d_lo
        for r in range(d_hi):
            for c in range(n_hi // COPY_ROWS):
                lo = pl.ds((r % d_lo) * n_lo + r // d_lo + ratio * COPY_ROWS * c, COPY_ROWS, stride=ratio)
                hi = pl.ds(r * n_hi + COPY_ROWS * c, COPY_ROWS)
                m1, m2 = m_lo[lo, :], m_hi[hi, :]
                m = jnp.maximum(m1, m2)
                a1, a2 = jnp.exp(m1 - m), jnp.exp(m2 - m)
                o_lo[lo, :] = o_lo[lo, :] * a1 + o_hi[hi, :] * a2
                l_lo[lo, :] = l_lo[lo, :] * a1 + l_hi[hi, :] * a2
                m_lo[lo, :] = m

    on, _, ln = accs[0]
    for c in range(seq // COPY_ROWS):
        rows = pl.ds(COPY_ROWS * c, COPY_ROWS)
        o_ref[rows, :] = on[rows, :] / ln[rows, :]


def _attn_prompt(q, k, v, batch, seq):
    width = q.shape[1]
    spec = pl.BlockSpec((seq, LANES), lambda b, g: (b, g))
    f32_scr = pltpu.VMEM((seq, LANES), F32)
    bf_scr = pltpu.VMEM((seq, LANES), BF16)
    return pl.pallas_call(
        _attn_prompt_kernel,
        grid=(batch, width // LANES),
        in_specs=[spec, spec, spec],
        out_specs=spec,
        out_shape=jax.ShapeDtypeStruct(q.shape, F32),
        scratch_shapes=[bf_scr] * 3 + [f32_scr] * (3 + 3 * len(DILATIONS)),
        compiler_params=_cparams(("arbitrary", "arbitrary")),
        name="attn_prompt",
    )(q, k, v)


KEY_PAD = 128
ROLL_ROWS = 64


def _attn_sample_kernel(windows, q_ref, k_ref, v_ref, ck_ref, cv_ref, o_ref, nk_ref, nv_ref):
    t_new, width = q_ref.shape[1], q_ref.shape[2]
    lb = ck_ref.shape[2]
    heads = width // HEAD_DIM
    rows = heads * t_new
    scale = HEAD_DIM ** -0.5
    k_new, v_new = k_ref[0], v_ref[0]
    pad = jnp.zeros((KEY_PAD - t_new, width), F32)

    lane = lax.broadcasted_iota(jnp.int32, (ROLL_ROWS, KEY_PAD), 1)
    for src, new, dst in ((ck_ref, k_new, nk_ref), (cv_ref, v_new, nv_ref)):
        tail = jnp.concatenate([pad, new], axis=0).T
        for c in range(width // ROLL_ROWS):
            rs = slice(ROLL_ROWS * c, ROLL_ROWS * (c + 1))
            y = pltpu.roll(src[0, rs, :], lb - t_new, 1)
            dst[0, rs, :lb - KEY_PAD] = y[:, :lb - KEY_PAD]
            dst[0, rs, lb - KEY_PAD:] = jnp.where(lane >= KEY_PAD - t_new, tail[rs, :], y[:, lb - KEY_PAD:])

    q_rep = jnp.concatenate([q_ref[0] * scale] * heads, axis=0)
    r_head = lax.broadcasted_iota(jnp.int32, (rows, width), 0) // t_new
    l_head = lax.broadcasted_iota(jnp.int32, (rows, width), 1) // HEAD_DIM
    own = r_head == l_head
    qbd = jnp.where(own, q_rep, 0.0).astype(BF16)
    k_pad = jnp.concatenate([k_new, pad], axis=0).astype(BF16)
    v_pad = jnp.concatenate([v_new, pad], axis=0).astype(BF16)
    s_cache = jnp.dot(qbd, ck_ref[0].astype(BF16), preferred_element_type=F32)
    s = jnp.concatenate([s_cache, _nt_dot(qbd, k_pad)], axis=1)
    tok = lax.broadcasted_iota(jnp.int32, s.shape, 0) % t_new
    key = lax.broadcasted_iota(jnp.int32, s.shape, 1)
    dist = lb + tok - key
    cnt = jnp.zeros(s.shape, F32)
    for w, d in windows:
        cnt = cnt + ((dist >= 0) & (dist <= w) & (dist % d == 0)).astype(F32)
    s = jnp.where(cnt > 0, s, -jnp.inf)
    m = jnp.max(s, axis=1, keepdims=True)
    p = (cnt * jnp.exp(s - m)).astype(BF16)
    l = jnp.sum(p.astype(F32), axis=1, keepdims=True)
    o = _nt_dot(p[:, :lb], cv_ref[0].astype(BF16)) + jnp.dot(p[:, lb:], v_pad, preferred_element_type=F32)
    o = jnp.where(own, o / l, 0.0).reshape(heads, t_new, width)
    o_ref[0] = jnp.sum(o, axis=0)


def _attn_sample(q, k, v, cache_kt, cache_vt, windows):
    batch, t_new, width = q.shape
    lb = cache_kt.shape[2]
    new_spec = pl.BlockSpec((1, t_new, width), lambda b: (b, 0, 0))
    cache_spec = pl.BlockSpec((1, width, lb), lambda b: (b, 0, 0))
    return pl.pallas_call(
        functools.partial(_attn_sample_kernel, windows),
        grid=(batch,),
        in_specs=[new_spec, new_spec, new_spec, cache_spec, cache_spec],
        out_specs=[new_spec, cache_spec, cache_spec],
        out_shape=[jax.ShapeDtypeStruct(q.shape, F32), jax.ShapeDtypeStruct(cache_kt.shape, F32),
                   jax.ShapeDtypeStruct(cache_vt.shape, F32)],
        compiler_params=_cparams(("arbitrary",)),
        name="attn_sample",
    )(q, k, v, cache_kt, cache_vt)


HALO = 32


def _layer_norm_rows(h, gain, bias):
    mu = jnp.mean(h, axis=-1, keepdims=True)
    hc = h - mu
    var = jnp.mean(hc * hc, axis=-1, keepdims=True)
    return hc * lax.rsqrt(var + LN_EPS) * gain + bias


def _mix_kernel(alpha, chunk, x_ref, att_ref, glu_ref, halo_ref, past_ref, wo_ref, cw_ref, cb_ref, cg_ref,
                cbeta_ref, g1_ref, b1_ref, x1_ref, xp, conv_scr, shifted):
    tm, conv_ch = glu_ref.shape
    first_tile = pl.program_id(1) == 0
    xp[0:HALO, :] = jnp.where(first_tile, past_ref[0], halo_ref[...])
    xp[HALO:HALO + tm, :] = glu_ref[...]
    lead = HALO - (CONV_K - 1)
    span = shifted.shape[1]
    for s in range(1, ROW_TILE):
        for r0 in range(0, span, COPY_ROWS):
            r1 = min(r0 + COPY_ROWS, span)
            shifted[s, r0:r1, :] = xp[r0 + s:r1 + s, :]
    for rc in range(tm // chunk):
        acc = jnp.zeros((chunk, conv_ch), F32)
        for tap in range(CONV_K):
            s, base = (lead + tap) % ROW_TILE, rc * chunk + (lead + tap) // ROW_TILE * ROW_TILE
            rows = xp[base:base + chunk, :] if s == 0 else shifted[s, base:base + chunk, :]
            acc = acc + cw_ref[tap:tap + 1, :] * rows
        y = _layer_norm_rows(acc + cb_ref[...], cg_ref[...], cbeta_ref[...])
        conv_scr[rc * chunk:(rc + 1) * chunk, :] = y * (1.0 / (1.0 + jnp.exp(-y)))
    att_w = att_ref.shape[1]
    mix = jnp.dot(att_ref[...].astype(BF16), wo_ref[0:att_w, :], preferred_element_type=F32)
    mix = mix + jnp.dot(conv_scr[...].astype(BF16), wo_ref[att_w:, :], preferred_element_type=F32)
    x1_ref[...] = _layer_norm_rows(alpha * x_ref[...] + mix, g1_ref[...], b1_ref[...])


def _mix(x, att, glu, past, w_out, conv_w, conv_b, conv_g, conv_beta, g1, b1, alpha, batch, seq, tm):
    n, d = x.shape
    att_w, conv_ch = att.shape[1], glu.shape[1]
    tiles = seq // tm
    past = jnp.pad(past, ((0, 0), (HALO - (CONV_K - 1), 0), (0, 0)))
    row = lambda w: pl.BlockSpec((tm, w), lambda b, i: (b * tiles + i, 0))
    vec = lambda w: pl.BlockSpec((1, w), lambda b, i: (0, 0))
    halo_blocks = n // HALO
    halo = pl.BlockSpec((HALO, conv_ch),
                        lambda b, i: (jnp.clip((b * seq + i * tm) // HALO - 1, 0, halo_blocks - 1), 0))
    return pl.pallas_call(
        functools.partial(_mix_kernel, alpha, min(tm, 128)),
        grid=(batch, tiles),
        in_specs=[row(d), row(att_w), row(conv_ch), halo,
                  pl.BlockSpec((1, HALO, conv_ch), lambda b, i: (b, 0, 0)),
                  pl.BlockSpec(w_out.shape, lambda b, i: (0, 0)),
                  pl.BlockSpec(conv_w.shape, lambda b, i: (0, 0)),
                  vec(conv_ch), vec(conv_ch), vec(conv_ch), vec(d), vec(d)],
        out_specs=row(d),
        out_shape=jax.ShapeDtypeStruct((n, d), F32),
        scratch_shapes=[pltpu.VMEM((HALO + tm, conv_ch), F32), pltpu.VMEM((tm, conv_ch), F32),
                        pltpu.VMEM((ROW_TILE, HALO + tm - ROW_TILE, conv_ch), F32)],
        compiler_params=_cparams(("arbitrary", "arbitrary")),
        name="mix",
    )(x, att, glu, glu, past, w_out, conv_w, conv_b.reshape(1, -1), conv_g.reshape(1, -1),
      conv_beta.reshape(1, -1), g1.reshape(1, -1), b1.reshape(1, -1))


def _scan_max(s_ref, rows_ref, prev):
    rows, tt = s_ref.shape
    prev = jnp.broadcast_to(prev, (ROW_TILE, tt))
    m = jnp.full((ROW_TILE, tt), -jnp.inf, F32)
    g = jnp.zeros((ROW_TILE, tt), F32)
    for j in range(rows // ROW_TILE):
        rs = slice(j * ROW_TILE, (j + 1) * ROW_TILE)
        sj = jnp.where(rows_ref[rs, :] == prev, -jnp.inf, s_ref[rs, :])
        s_ref[rs, :] = sj
        gt = sj > m
        m = jnp.where(gt, sj, m)
        g = jnp.where(gt, float(j), g)
    key = g * float(ROW_TILE) + rows_ref[0:ROW_TILE, :]
    top = jnp.max(m, axis=0, keepdims=True)
    idx = jnp.min(jnp.where(m == top, key, float(rows)), axis=0, keepdims=True)
    return top, idx


def _route_kernel(half_experts, x_ref, wq_ref, sk_ref, m_ref, hi_ref, gate_ref, v0, i0, v1, i1, bs, be, cs, ce,
                  sa, sb, key_rows):
    tt = x_ref.shape[0]
    q = jnp.dot(x_ref[...].astype(BF16), wq_ref[...], preferred_element_type=F32)
    qb = q.astype(BF16)
    sa[...] = _nt_dot(sk_ref[0, 0], qb[:, :LANES])
    sb[...] = _nt_dot(sk_ref[0, 1], qb[:, LANES:])
    key_rows[...] = lax.broadcasted_iota(jnp.int32, (N_KEYS, tt), 0).astype(F32)

    def first(r, prev):
        ma, ia = _scan_max(sa, key_rows, prev[0])
        mb, ib = _scan_max(sb, key_rows, prev[1])
        v0[pl.ds(r, 1), :] = ma
        i0[pl.ds(r, 1), :] = ia
        v1[pl.ds(r, 1), :] = mb
        i1[pl.ds(r, 1), :] = ib
        return ia, ib

    none = jnp.full((1, tt), -1.0, F32)
    lax.fori_loop(0, PEER_TOPK, first, (none, none))

    off = 0
    for a in range(PEER_TOPK):
        nb = PEER_TOPK // (a + 1)
        cs[off:off + nb, :] = v0[a:a + 1, :] + v1[0:nb, :]
        ce[off:off + nb, :] = i0[a:a + 1, :] * float(N_KEYS) + i1[0:nb, :]
        off += nb
    cs[off:, :] = jnp.full((cs.shape[0] - off, tt), -jnp.inf, F32)
    ce[off:, :] = jnp.full((cs.shape[0] - off, tt), -1.0, F32)
    pairs = PEER_TOPK // 2
    cand_rows = cs.shape[0]

    def second(r, prev):
        m, idx = _scan_max(cs, key_rows, prev)
        row = lax.shift_right_logical(r, 1) + pairs * (r & 1)
        bs[pl.ds(row, 1), :] = m
        picked = jnp.where(key_rows[0:cand_rows, :] == idx, ce[...], -1.0)
        be[pl.ds(row, 1), :] = jnp.max(picked, axis=0, keepdims=True)
        return idx

    lax.fori_loop(0, PEER_TOPK, second, none)

    best = bs[...]
    p = jnp.exp(best - jnp.max(best, axis=0, keepdims=True))
    gate = p / jnp.sum(p, axis=0, keepdims=True)
    e = be[...].astype(jnp.int32)
    high = (e >= half_experts).astype(jnp.int32)
    rows = (e - high * half_experts) * ROW_TILE
    words = rows[:pairs] | (rows[pairs:] << 16)
    tb = m_ref.shape[2]
    for k in range(tt // tb):
        lanes = slice(k * tb, (k + 1) * tb)
        gate_ref[k] = gate[:, lanes]
        hi_ref[k] = high[:, lanes]
        m_ref[k] = words[:, lanes]


def _route(x, wq, sk, tt, tb):
    n, d = x.shape
    half_experts = N_KEYS * N_KEYS // 2
    cand_rows = -(-sum(PEER_TOPK // (a + 1) for a in range(PEER_TOPK)) // ROW_TILE) * ROW_TILE
    out = jax.ShapeDtypeStruct((n // tb, SLOTS, tb), jnp.int32)
    ospec = pl.BlockSpec((tt // tb, PEER_TOPK, tb), lambda i, h: (i, h, 0))
    return pl.pallas_call(
        functools.partial(_route_kernel, half_experts),
        grid=(n // tt, PEER_HEADS),
        in_specs=[
            pl.BlockSpec((tt, d), lambda i, h: (i, 0)),
            pl.BlockSpec((d, 2 * LANES), lambda i, h: (0, h)),
            pl.BlockSpec((1, 2, N_KEYS, LANES), lambda i, h: (h, 0, 0, 0)),
        ],
        out_specs=[pl.BlockSpec((tt // tb, PEER_TOPK // 2, tb), lambda i, h: (i, h, 0)), ospec, ospec],
        out_shape=[jax.ShapeDtypeStruct((n // tb, SLOTS // 2, tb), jnp.int32), out,
                   jax.ShapeDtypeStruct(out.shape, F32)],
        scratch_shapes=([pltpu.VMEM((PEER_TOPK, tt), F32)] * 6 + [pltpu.VMEM((cand_rows, tt), F32)] * 2
                        + [pltpu.VMEM((N_KEYS, tt), F32)] * 3),
        compiler_params=_cparams(("arbitrary", "arbitrary")),
        name="peer_route",
    )(x, wq, sk)


def _bf16_bits(x):
    return pltpu.bitcast(x.astype(BF16).astype(F32), jnp.uint32)


def _pack_kernel(lo_ref, hi_ref, o_ref):
    half_rows = ROW_TILE // 2
    for i in range(lo_ref.shape[0] // ROW_TILE):
        src = slice(i * ROW_TILE, (i + 1) * ROW_TILE)
        for part, ref in enumerate((lo_ref, hi_ref)):
            for q in range(half_rows):
                a = _bf16_bits(ref[src, 2 * q * LANES:(2 * q + 1) * LANES])
                b = _bf16_bits(ref[src, (2 * q + 1) * LANES:(2 * q + 2) * LANES])
                word = b | lax.shift_right_logical(a, jnp.uint32(16))
                rows = pl.ds(i * ROW_TILE * ROW_TILE + part * half_rows + q, ROW_TILE, stride=ROW_TILE)
                o_ref[rows, :] = pltpu.bitcast(word, jnp.int32)


def _pack_table(tab, rows=PACK_TILE):
    n, d = tab.shape
    half = n // 2
    steps = half // rows
    return pl.pallas_call(
        _pack_kernel,
        grid=(steps,),
        in_specs=[pl.BlockSpec((rows, d), lambda i: (i, 0)),
                  pl.BlockSpec((rows, d), lambda i: (i + steps, 0))],
        out_specs=pl.BlockSpec((rows * ROW_TILE, LANES), lambda i: (i, 0)),
        out_shape=jax.ShapeDtypeStruct((half * ROW_TILE, LANES), jnp.int32),
        compiler_params=_cparams(("arbitrary",)),
        name="pack_table",
    )(tab, tab)


def _load_tile(tab_ref, row):
    return pltpu.bitcast(tab_ref[pl.ds(pl.multiple_of(row, ROW_TILE), ROW_TILE), :], BF16)


def _load_chunk(tab_ref, m_ref, c, t, tb):
    words = [m_ref[(c * (CHUNK // 2) + j) * tb + t] for j in range(CHUNK // 2)]
    return ([_load_tile(tab_ref, w & 0xFFFF) for w in words]
            + [_load_tile(tab_ref, lax.shift_right_logical(w, 16)) for w in words])


def _diag_mask(rows):
    r = lax.broadcasted_iota(jnp.int32, (rows, CHUNK * PACK_ROWS), 0)
    c = lax.broadcasted_iota(jnp.int32, (rows, CHUNK * PACK_ROWS), 1)
    return (r % PACK_ROWS) == (c % PACK_ROWS)


def _peer_act_kernel(m_ref, x_ref, hi_ref, gate_ref, sel_ref, tab_ref, coef_ref, r_scr):
    tb = x_ref.shape[0]
    diag = _diag_mask(2 * PACK_ROWS)

    def tokens(it, carry):
        for u in range(TOKEN_UNROLL):
            t = it * TOKEN_UNROLL + u
            xh, xl = _split_bf16(x_ref[t])
            lhs = jnp.concatenate([xh, xh, xl, xl] * ACT_LHS_COPIES, axis=0)
            for c in range(SLOTS // CHUNK):
                tiles = _load_chunk(tab_ref, m_ref, c, t, tb)
                g = jnp.concatenate(tiles, axis=0)
                o = jnp.where(diag, _nt_dot(lhs, g)[:2 * PACK_ROWS], 0.0)
                r_scr[pl.ds(t, 1), c * 256:(c + 1) * 256] = jnp.sum(o, axis=0, keepdims=True)
        return carry

    lax.fori_loop(0, tb // TOKEN_UNROLL, tokens, 0)
    act2 = _dot_select(r_scr[...], sel_ref[...])
    high = hi_ref[0].astype(F32).T > 0
    act = jnp.where(high, act2[:, SLOTS:], act2[:, :SLOTS])
    gelu = 0.5 * act * (1.0 + lax.erf(act * (1.0 / math.sqrt(2.0))))
    coef_ref[...] = gate_ref[0].T * gelu


def _peer_out_kernel(alpha, m_ref, x_ref, hi_ref, coef_ref, exp_ref, g_ref, b_ref, tab_ref, y_ref, ce_scr, h_scr):
    tb = x_ref.shape[0]
    diag = _diag_mask(PACK_ROWS)
    coef = coef_ref[...]
    high = hi_ref[0].astype(F32).T > 0
    c2 = jnp.concatenate([jnp.where(high, 0.0, coef), jnp.where(high, coef, 0.0)], axis=1)
    ce_scr[...] = _dot_select(c2, exp_ref[...])
    def tokens(it, carry):
        for u in range(TOKEN_UNROLL):
            t = it * TOKEN_UNROLL + u
            acc = jnp.zeros((PACK_ROWS, LANES), F32)
            for c in range(SLOTS // CHUNK):
                tiles = _load_chunk(tab_ref, m_ref, c, t, tb)
                g = jnp.concatenate(tiles, axis=0)
                ce = ce_scr[pl.ds(t, 1), c * 256:(c + 1) * 256]
                cm = jnp.where(diag, jnp.broadcast_to(ce, (PACK_ROWS, 256)), 0.0)
                ch, cl = _split_bf16(cm)
                lhs = jnp.concatenate([ch, cl] * OUT_LHS_COPIES, axis=0)
                o = jnp.dot(lhs, g, preferred_element_type=F32)[:2 * PACK_ROWS]
                acc = acc + o[:PACK_ROWS] + o[PACK_ROWS:]
            rows = pl.ds(pl.multiple_of(t * ROW_TILE, ROW_TILE), ROW_TILE)
            h_scr[rows, :] = alpha * x_ref[t] + acc[:ROW_TILE] + acc[ROW_TILE:]
        return carry

    lax.fori_loop(0, tb // TOKEN_UNROLL, tokens, 0)

    h = jnp.concatenate([h_scr[pl.ds(j, tb, stride=ROW_TILE), :] for j in range(ROW_TILE)], axis=1)
    y_ref[...] = _layer_norm_rows(h, g_ref[...], b_ref[...])


def _exp_matrix():
    return _sel_matrix().T


def _peer_out(m_flat, x3, hi, coef, tab, gain, bias, alpha, tb):
    n = x3.shape[0]
    d = ROW_TILE * LANES
    expand = _exp_matrix()
    return pl.pallas_call(
        functools.partial(_peer_out_kernel, alpha),
        grid=(n // tb,),
        in_specs=[
            pl.BlockSpec((tb * SLOTS // 2,), lambda i: (i,), memory_space=pltpu.SMEM),
            pl.BlockSpec((tb, ROW_TILE, LANES), lambda i: (i, 0, 0)),
            pl.BlockSpec((1, SLOTS, tb), lambda i: (i, 0, 0)),
            pl.BlockSpec((tb, SLOTS), lambda i: (i, 0)),
            pl.BlockSpec(expand.shape, lambda i: (0, 0)),
            pl.BlockSpec((1, d), lambda i: (0, 0)),
            pl.BlockSpec((1, d), lambda i: (0, 0)),
            pl.BlockSpec(tab.shape, lambda i: (0, 0), pipeline_mode=pl.Buffered(1)),
        ],
        out_specs=pl.BlockSpec((tb, d), lambda i: (i, 0)),
        out_shape=jax.ShapeDtypeStruct((n, d), F32),
        scratch_shapes=[pltpu.VMEM((tb, SLOTS * PACK_ROWS), F32), pltpu.VMEM((tb * ROW_TILE, LANES), F32)],
        compiler_params=_cparams(("arbitrary",)),
        name="peer_out",
    )(m_flat, x3, hi, coef, expand, gain.reshape(1, d), bias.reshape(1, d), tab)


def _sel_matrix():
    k = jnp.arange(SLOTS * PACK_ROWS)
    slot, row = k // PACK_ROWS, k % PACK_ROWS
    col = jnp.where(row < ROW_TILE, slot, SLOTS + slot)
    return (col[:, None] == jnp.arange(2 * SLOTS)[None, :]).astype(BF16)


def _peer_act(m_flat, x3, hi, gate, tab, tb):
    n = x3.shape[0]
    sel = _sel_matrix()
    return pl.pallas_call(
        _peer_act_kernel,
        grid=(n // tb,),
        in_specs=[
            pl.BlockSpec((tb * SLOTS // 2,), lambda i: (i,), memory_space=pltpu.SMEM),
            pl.BlockSpec((tb, ROW_TILE, LANES), lambda i: (i, 0, 0)),
            pl.BlockSpec((1, SLOTS, tb), lambda i: (i, 0, 0)),
            pl.BlockSpec((1, SLOTS, tb), lambda i: (i, 0, 0)),
            pl.BlockSpec(sel.shape, lambda i: (0, 0)),
            pl.BlockSpec(tab.shape, lambda i: (0, 0), pipeline_mode=pl.Buffered(1)),
        ],
        out_specs=pl.BlockSpec((tb, SLOTS), lambda i: (i, 0)),
        out_shape=jax.ShapeDtypeStruct((n, SLOTS), F32),
        scratch_shapes=[pltpu.VMEM((tb, SLOTS * PACK_ROWS), F32)],
        compiler_params=_cparams(("arbitrary",)),
        name="peer_act",
    )(m_flat, x3, hi, gate, sel, tab)


def _peer_layer(x1, wq, sk, u_tab, v_tab, gain, bias, alpha):
    n = x1.shape[0]
    m, hi, gate = _route(x1, wq, sk, min(ROUTE_TILE, n), PEER_TILE)
    m_flat = m.reshape(n * SLOTS // 2)
    x3 = x1.reshape(n, ROW_TILE, LANES)
    coef = _peer_act(m_flat, x3, hi, gate, u_tab, PEER_TILE)
    return _peer_out(m_flat, x3, hi, coef, v_tab, gain, bias, alpha, PEER_TILE)


ROUTE_TILE = 512
PEER_TILE = 256
PROJ_TILE = 256
MIX_TILE = 256


def kernel(x_prompt, x_sample, cache_k_win, cache_v_win, state_conv, w_in, w_out, conv_w, conv_b, conv_ln_g,
           conv_ln_b, ln1_g, ln1_b, w_query, sub_keys, expert_u, expert_v, ln2_g, ln2_b):
    depth, d_model, in_cols = w_in.shape
    batch, seq, _ = x_prompt.shape
    dec_batch, dec_seq, _ = x_sample.shape
    conv_ch = conv_w.shape[2]
    att_w = (in_cols - 2 * conv_ch) // 3
    heads = att_w // HEAD_DIM
    lb = cache_k_win.shape[2]
    windows = tuple((WIN_STEPS * d, d) for d in DILATIONS)
    w_max = windows[-1][0]
    assert lb == w_max and seq >= w_max and d_model == ROW_TILE * LANES
    alpha = (2.0 * depth) ** 0.25

    hp = x_prompt.reshape(batch * seq, d_model)
    hs = x_sample.reshape(dec_batch * dec_seq, d_model)
    outs = [[] for _ in range(6)]
    for l in range(depth):
        w_in_l, w_out_l = w_in[l].astype(BF16), w_out[l].astype(BF16)
        wq_l, sk_l = w_query[l].astype(BF16), sub_keys[l].astype(BF16)
        u_tab, v_tab = _pack_table(expert_u[l]), _pack_table(expert_v[l])
        mix_args = (w_out_l, conv_w[l], conv_b[l], conv_ln_g[l], conv_ln_b[l], ln1_g[l], ln1_b[l], alpha)
        peer_args = (wq_l, sk_l, u_tab, v_tab, ln2_g[l], ln2_b[l], alpha)

        q, k, v, glu, kt, vt = _in_proj(hp, w_in_l, att_w, PROJ_TILE, seq, w_max)
        att = _attn_prompt(q, k, v, batch, seq)
        past = jnp.zeros((batch, CONV_K - 1, conv_ch), F32)
        x1 = _mix(hp, att, glu, past, *mix_args, batch, seq, MIX_TILE)
        outs[0].append(kt.reshape(batch, heads, HEAD_DIM, w_max).transpose(0, 3, 1, 2))
        outs[1].append(vt.reshape(batch, heads, HEAD_DIM, w_max).transpose(0, 3, 1, 2))
        outs[2].append(glu.reshape(batch, seq, conv_ch)[:, seq - (CONV_K - 1):])
        hp = _peer_layer(x1, *peer_args)

        q, k, v, glu = _in_proj(hs, w_in_l, att_w, dec_batch * dec_seq)
        new3 = lambda a: a.reshape(dec_batch, dec_seq, att_w)
        to_feature_major = lambda c: c.transpose(0, 2, 3, 1).reshape(dec_batch, att_w, lb)
        att, nk, nv = _attn_sample(new3(q), new3(k), new3(v), to_feature_major(cache_k_win[l]),
                                   to_feature_major(cache_v_win[l]), windows)
        x1 = _mix(hs, att.reshape(dec_batch * dec_seq, att_w), glu, state_conv[l], *mix_args,
                  dec_batch, dec_seq, dec_seq)
        conv_in = jnp.concatenate([state_conv[l], glu.reshape(dec_batch, dec_seq, conv_ch)], axis=1)
        to_position_major = lambda c: c.reshape(dec_batch, heads, HEAD_DIM, lb).transpose(0, 3, 1, 2)
        outs[3].append(to_position_major(nk))
        outs[4].append(to_position_major(nv))
        outs[5].append(conv_in[:, dec_seq:])
        hs = _peer_layer(x1, *peer_args)

    stack = lambda rows: jnp.stack(rows, 0)
    return (hp.reshape(batch, seq, d_model), hs.reshape(dec_batch, dec_seq, d_model),
            stack(outs[0]), stack(outs[1]), stack(outs[2]), stack(outs[3]), stack(outs[4]), stack(outs[5]))
```

```python
import functools
import math

import jax
import jax.numpy as jnp
from jax import lax
from jax.experimental import pallas as pl
from jax.experimental.pallas import tpu as pltpu

F32 = jnp.float32
BF16 = jnp.bfloat16

HEAD_DIM = 64
N_KEYS = 128
PEER_HEADS = 8
PEER_TOPK = 16
SLOTS = PEER_HEADS * PEER_TOPK
CONV_K = 31
DILATIONS = (1, 4, 16)
WIN_STEPS = 128
LN_EPS = 1e-5

LANES = 128
ROW_TILE = 8
PACK_ROWS = 16
CHUNK = 16
TOKEN_UNROLL = 32
ACT_LHS_COPIES = 4
OUT_LHS_COPIES = 3
VMEM_LIMIT = 56 * 1024 * 1024
PACK_TILE = 512


def _cparams(sem):
    return pltpu.CompilerParams(dimension_semantics=sem, vmem_limit_bytes=VMEM_LIMIT)


def _nt_dot(a, b):
    return lax.dot_general(a, b, (((1,), (1,)), ((), ())), preferred_element_type=F32)


def _split_bf16(x):
    hi = x.astype(BF16)
    lo = (x - hi.astype(F32)).astype(BF16)
    return hi, lo


def _dot_select(a, sel):
    hi = a.astype(BF16)
    rest = a - hi.astype(F32)
    mid = rest.astype(BF16)
    lo = (rest - mid.astype(F32)).astype(BF16)
    dot = lambda p: jnp.dot(p, sel, preferred_element_type=F32)
    return dot(hi) + dot(mid) + dot(lo)


def _in_proj_kernel(att_w, window, x_ref, w_ref, q_ref, k_ref, v_ref, glu_ref, *kv_t_refs):
    z = jnp.dot(x_ref[...].astype(BF16), w_ref[...], preferred_element_type=F32)
    q_ref[...] = z[:, :att_w]
    k_ref[...] = z[:, att_w:2 * att_w]
    v_ref[...] = z[:, 2 * att_w:3 * att_w]
    if window is not None:
        tiles_per_seq, first = window
        kt_ref, vt_ref = kv_t_refs

        @pl.when(pl.program_id(0) % tiles_per_seq >= first)
        def _():
            kt_ref[0] = z[:, att_w:2 * att_w].T
            vt_ref[0] = z[:, 2 * att_w:3 * att_w].T
    conv_ch = (z.shape[1] - 3 * att_w) // 2
    a = z[:, 3 * att_w:3 * att_w + conv_ch]
    g = z[:, 3 * att_w + conv_ch:]
    glu_ref[...] = a * (1.0 / (1.0 + jnp.exp(-g)))


def _in_proj(x, w_in, att_w, tm, seq=None, keep=None):
    n, d = x.shape
    cols = w_in.shape[1]
    conv_ch = (cols - 3 * att_w) // 2
    out = lambda w: jax.ShapeDtypeStruct((n, w), F32)
    ospec = lambda w: pl.BlockSpec((tm, w), lambda i: (i, 0))
    out_specs = [ospec(att_w), ospec(att_w), ospec(att_w), ospec(conv_ch)]
    out_shape = [out(att_w), out(att_w), out(att_w), out(conv_ch)]
    window = None
    if keep is not None:
        tiles_per_seq, first = seq // tm, (seq - keep) // tm
        window = (tiles_per_seq, first)
        wspec = pl.BlockSpec((1, att_w, tm),
                             lambda i: (i // tiles_per_seq, 0, jnp.maximum(i % tiles_per_seq - first, 0)))
        out_specs += [wspec, wspec]
        out_shape += [jax.ShapeDtypeStruct((n // seq, att_w, keep), F32)] * 2
    return pl.pallas_call(
        functools.partial(_in_proj_kernel, att_w, window),
        grid=(n // tm,),
        in_specs=[pl.BlockSpec((tm, d), lambda i: (i, 0)),
                  pl.BlockSpec((d, cols), lambda i: (0, 0))],
        out_specs=out_specs,
        out_shape=out_shape,
        compiler_params=_cparams(("arbitrary",)),
        name="in_proj",
    )(x, w_in)


Q_BLOCK = 128
ATTN_UNROLL = 8
COPY_ROWS = 256


def _attn_prompt_kernel(q_ref, k_ref, v_ref, o_ref, qs, ks, vs, qf, kf, vf, *acc_refs):
    seq = q_ref.shape[0]
    accs = [acc_refs[3 * b:3 * b + 3] for b in range(len(DILATIONS))]
    scale = HEAD_DIM ** -0.5
    lane = lax.broadcasted_iota(jnp.int32, (1, LANES), 1)
    first_head = lane < HEAD_DIM
    qi = lax.broadcasted_iota(jnp.int32, (Q_BLOCK, 2 * Q_BLOCK), 0)
    kj = lax.broadcasted_iota(jnp.int32, (Q_BLOCK, 2 * Q_BLOCK), 1)
    rel = qi - kj

    for level, (d, (o_dst, m_dst, l_dst)) in enumerate(zip(DILATIONS, accs)):
        n_sub = seq // d
        blocks_per_sub = n_sub // Q_BLOCK
        d_prev = DILATIONS[level - 1] if level else 1
        n_prev, ratio = seq // d_prev, d // d_prev
        srcs = (q_ref, k_ref, v_ref) if level < 2 else (qf, kf, vf)
        keep_f32 = 0 < level < len(DILATIONS) - 1
        for r in range(d):
            for c in range(n_sub // COPY_ROWS):
                start = (r % d_prev) * n_prev + r // d_prev + ratio * COPY_ROWS * c
                src = pl.ds(start, COPY_ROWS, stride=ratio) if ratio > 1 else pl.ds(start, COPY_ROWS)
                dst = pl.ds(r * n_sub + COPY_ROWS * c, COPY_ROWS)
                rows = [ref[src, :] for ref in srcs]
                if keep_f32:
                    for ref, x in zip((qf, kf, vf), rows):
                        ref[dst, :] = x
                qs[dst, :] = (rows[0] * scale).astype(BF16)
                ks[dst, :] = rows[1].astype(BF16)
                vs[dst, :] = rows[2].astype(BF16)

        def one_block(i):
            il = i % blocks_per_sub
            has_prev = il > 0
            row0 = pl.multiple_of(i * Q_BLOCK, Q_BLOCK)
            kstart = pl.multiple_of(jnp.where(has_prev, row0 - Q_BLOCK, row0), Q_BLOCK)
            delta = rel + jnp.where(has_prev, Q_BLOCK, 0)
            valid = (delta >= 0) & (delta <= WIN_STEPS)
            qb = qs[pl.ds(row0, Q_BLOCK), :]
            kb = ks[pl.ds(kstart, 2 * Q_BLOCK), :]
            vb = vs[pl.ds(kstart, 2 * Q_BLOCK), :]
            outs = []
            for head_mask in (first_head, jnp.logical_not(first_head)):
                qh = jnp.where(head_mask, qb, jnp.zeros_like(qb))
                s = jnp.where(valid, _nt_dot(qh, kb), -jnp.inf)
                m = jnp.max(s, axis=1, keepdims=True)
                p = jnp.exp(s - m)
                l = jnp.sum(p, axis=1, keepdims=True)
                o = jnp.dot(p.astype(BF16), vb, preferred_element_type=F32)
                outs.append((o, m, l))
            (o0, m0, l0), (o1, m1, l1) = outs
            rows = pl.ds(row0, Q_BLOCK)
            o_dst[rows, :] = jnp.where(first_head, o0, o1)
            m_dst[rows, :] = jnp.where(first_head, m0, m1)
            l_dst[rows, :] = jnp.where(first_head, l0, l1)

        def blocks(it, carry):
            for u in range(ATTN_UNROLL):
                one_block(it * ATTN_UNROLL + u)
            return carry

        lax.fori_loop(0, seq // (Q_BLOCK * ATTN_UNROLL), blocks, 0)

    for level in range(len(DILATIONS) - 1, 0, -1):
        d_hi, d_lo = DILATIONS[level], DILATIONS[level - 1]
        (o_hi, m_hi, l_hi), (o_lo, m_lo, l_lo) = accs[level], accs[level - 1]
        n_hi, n_lo, ratio = seq // d_hi, seq // d_lo, d_hi // d_lo
        for r in range(d_hi):
            for c in range(n_hi // COPY_ROWS):
                lo = pl.ds((r % d_lo) * n_lo + r // d_lo + ratio * COPY_ROWS * c, COPY_ROWS, stride=ratio)
                hi = pl.ds(r * n_hi + COPY_ROWS * c, COPY_ROWS)
                m1, m2 = m_lo[lo, :], m_hi[hi, :]
                m = jnp.maximum(m1, m2)
                a1, a2 = jnp.exp(m1 - m), jnp.exp(m2 - m)
                o_lo[lo, :] = o_lo[lo, :] * a1 + o_hi[hi, :] * a2
                l_lo[lo, :] = l_lo[lo, :] * a1 + l_hi[hi, :] * a2
                m_lo[lo, :] = m

    on, _, ln = accs[0]
    for c in range(seq // COPY_ROWS):
        rows = pl.ds(COPY_ROWS * c, COPY_ROWS)
        o_ref[rows, :] = on[rows, :] / ln[rows, :]


def _attn_prompt(q, k, v, batch, seq):
    width = q.shape[1]
    spec = pl.BlockSpec((seq, LANES), lambda b, g: (b, g))
    f32_scr = pltpu.VMEM((seq, LANES), F32)
    bf_scr = pltpu.VMEM((seq, LANES), BF16)
    return pl.pallas_call(
        _attn_prompt_kernel,
        grid=(batch, width // LANES),
        in_specs=[spec, spec, spec],
        out_specs=spec,
        out_shape=jax.ShapeDtypeStruct(q.shape, F32),
        scratch_shapes=[bf_scr] * 3 + [f32_scr] * (3 + 3 * len(DILATIONS)),
        compiler_params=_cparams(("arbitrary", "arbitrary")),
        name="attn_prompt",
    )(q, k, v)


KEY_PAD = 128
ROLL_ROWS = 64


def _attn_sample_kernel(windows, q_ref, k_ref, v_ref, ck_ref, cv_ref, o_ref, nk_ref, nv_ref):
    t_new, width = q_ref.shape[1], q_ref.shape[2]
    lb = ck_ref.shape[2]
    heads = width // HEAD_DIM
    rows = heads * t_new
    scale = HEAD_DIM ** -0.5
    k_new, v_new = k_ref[0], v_ref[0]
    pad = jnp.zeros((KEY_PAD - t_new, width), F32)

    lane = lax.broadcasted_iota(jnp.int32, (ROLL_ROWS, KEY_PAD), 1)
    for src, new, dst in ((ck_ref, k_new, nk_ref), (cv_ref, v_new, nv_ref)):
        tail = jnp.concatenate([pad, new], axis=0).T
        for c in range(width // ROLL_ROWS):
            rs = slice(ROLL_ROWS * c, ROLL_ROWS * (c + 1))
            y = pltpu.roll(src[0, rs, :], lb - t_new, 1)
            dst[0, rs, :lb - KEY_PAD] = y[:, :lb - KEY_PAD]
            dst[0, rs, lb - KEY_PAD:] = jnp.where(lane >= KEY_PAD - t_new, tail[rs, :], y[:, lb - KEY_PAD:])

    q_rep = jnp.concatenate([q_ref[0] * scale] * heads, axis=0)
    r_head = lax.broadcasted_iota(jnp.int32, (rows, width), 0) // t_new
    l_head = lax.broadcasted_iota(jnp.int32, (rows, width), 1) // HEAD_DIM
    own = r_head == l_head
    qbd = jnp.where(own, q_rep, 0.0).astype(BF16)
    k_pad = jnp.concatenate([k_new, pad], axis=0).astype(BF16)
    v_pad = jnp.concatenate([v_new, pad], axis=0).astype(BF16)
    s_cache = jnp.dot(qbd, ck_ref[0].astype(BF16), preferred_element_type=F32)
    s = jnp.concatenate([s_cache, _nt_dot(qbd, k_pad)], axis=1)
    tok = lax.broadcasted_iota(jnp.int32, s.shape, 0) % t_new
    key = lax.broadcasted_iota(jnp.int32, s.shape, 1)
    dist = lb + tok - key
    cnt = jnp.zeros(s.shape, F32)
    for w, d in windows:
        cnt = cnt + ((dist >= 0) & (dist <= w) & (dist % d == 0)).astype(F32)
    s = jnp.where(cnt > 0, s, -jnp.inf)
    m = jnp.max(s, axis=1, keepdims=True)
    p = (cnt * jnp.exp(s - m)).astype(BF16)
    l = jnp.sum(p.astype(F32), axis=1, keepdims=True)
    o = _nt_dot(p[:, :lb], cv_ref[0].astype(BF16)) + jnp.dot(p[:, lb:], v_pad, preferred_element_type=F32)
    o = jnp.where(own, o / l, 0.0).reshape(heads, t_new, width)
    o_ref[0] = jnp.sum(o, axis=0)


def _attn_sample(q, k, v, cache_kt, cache_vt, windows):
    batch, t_new, width = q.shape
    lb = cache_kt.shape[2]
    new_spec = pl.BlockSpec((1, t_new, width), lambda b: (b, 0, 0))
    cache_spec = pl.BlockSpec((1, width, lb), lambda b: (b, 0, 0))
    return pl.pallas_call(
        functools.partial(_attn_sample_kernel, windows),
        grid=(batch,),
        in_specs=[new_spec, new_spec, new_spec, cache_spec, cache_spec],
        out_specs=[new_spec, cache_spec, cache_spec],
        out_shape=[jax.ShapeDtypeStruct(q.shape, F32), jax.ShapeDtypeStruct(cache_kt.shape, F32),
                   jax.ShapeDtypeStruct(cache_vt.shape, F32)],
        compiler_params=_cparams(("arbitrary",)),
        name="attn_sample",
    )(q, k, v, cache_kt, cache_vt)


HALO = 32


def _layer_norm_rows(h, gain, bias):
    mu = jnp.mean(h, axis=-1, keepdims=True)
    hc = h - mu
    var = jnp.mean(hc * hc, axis=-1, keepdims=True)
    return hc * lax.rsqrt(var + LN_EPS) * gain + bias


def _mix_kernel(alpha, chunk, x_ref, att_ref, glu_ref, halo_ref, past_ref, wo_ref, cw_ref, cb_ref, cg_ref,
                cbeta_ref, g1_ref, b1_ref, x1_ref, xp, conv_scr, shifted):
    tm, conv_ch = glu_ref.shape
    first_tile = pl.program_id(1) == 0
    xp[0:HALO, :] = jnp.where(first_tile, past_ref[0], halo_ref[...])
    xp[HALO:HALO + tm, :] = glu_ref[...]
    lead = HALO - (CONV_K - 1)
    span = shifted.shape[1]
    for s in range(1, ROW_TILE):
        for r0 in range(0, span, COPY_ROWS):
            r1 = min(r0 + COPY_ROWS, span)
            shifted[s, r0:r1, :] = xp[r0 + s:r1 + s, :]
    for rc in range(tm // chunk):
        acc = jnp.zeros((chunk, conv_ch), F32)
        for tap in range(CONV_K):
            s, base = (lead + tap) % ROW_TILE, rc * chunk + (lead + tap) // ROW_TILE * ROW_TILE
            rows = xp[base:base + chunk, :] if s == 0 else shifted[s, base:base + chunk, :]
            acc = acc + cw_ref[tap:tap + 1, :] * rows
        y = _layer_norm_rows(acc + cb_ref[...], cg_ref[...], cbeta_ref[...])
        conv_scr[rc * chunk:(rc + 1) * chunk, :] = y * (1.0 / (1.0 + jnp.exp(-y)))
    att_w = att_ref.shape[1]
    mix = jnp.dot(att_ref[...].astype(BF16), wo_ref[0:att_w, :], preferred_element_type=F32)
    mix = mix + jnp.dot(conv_scr[...].astype(BF16), wo_ref[att_w:, :], preferred_element_type=F32)
    x1_ref[...] = _layer_norm_rows(alpha * x_ref[...] + mix, g1_ref[...], b1_ref[...])


def _mix(x, att, glu, past, w_out, conv_w, conv_b, conv_g, conv_beta, g1, b1, alpha, batch, seq, tm):
    n, d = x.shape
    att_w, conv_ch = att.shape[1], glu.shape[1]
    tiles = seq // tm
    past = jnp.pad(past, ((0, 0), (HALO - (CONV_K - 1), 0), (0, 0)))
    row = lambda w: pl.BlockSpec((tm, w), lambda b, i: (b * tiles + i, 0))
    vec = lambda w: pl.BlockSpec((1, w), lambda b, i: (0, 0))
    halo_blocks = n // HALO
    halo = pl.BlockSpec((HALO, conv_ch),
                        lambda b, i: (jnp.clip((b * seq + i * tm) // HALO - 1, 0, halo_blocks - 1), 0))
    return pl.pallas_call(
        functools.partial(_mix_kernel, alpha, min(tm, 128)),
        grid=(batch, tiles),
        in_specs=[row(d), row(att_w), row(conv_ch), halo,
                  pl.BlockSpec((1, HALO, conv_ch), lambda b, i: (b, 0, 0)),
                  pl.BlockSpec(w_out.shape, lambda b, i: (0, 0)),
                  pl.BlockSpec(conv_w.shape, lambda b, i: (0, 0)),
                  vec(conv_ch), vec(conv_ch), vec(conv_ch), vec(d), vec(d)],
        out_specs=row(d),
        out_shape=jax.ShapeDtypeStruct((n, d), F32),
        scratch_shapes=[pltpu.VMEM((HALO + tm, conv_ch), F32), pltpu.VMEM((tm, conv_ch), F32),
                        pltpu.VMEM((ROW_TILE, HALO + tm - ROW_TILE, conv_ch), F32)],
        compiler_params=_cparams(("arbitrary", "arbitrary")),
        name="mix",
    )(x, att, glu, glu, past, w_out, conv_w, conv_b.reshape(1, -1), conv_g.reshape(1, -1),
      conv_beta.reshape(1, -1), g1.reshape(1, -1), b1.reshape(1, -1))


def _scan_max(s_ref, rows_ref, prev):
    rows, tt = s_ref.shape
    prev = jnp.broadcast_to(prev, (ROW_TILE, tt))
    m = jnp.full((ROW_TILE, tt), -jnp.inf, F32)
    g = jnp.zeros((ROW_TILE, tt), F32)
    for j in range(rows // ROW_TILE):
        rs = slice(j * ROW_TILE, (j + 1) * ROW_TILE)
        sj = jnp.where(rows_ref[rs, :] == prev, -jnp.inf, s_ref[rs, :])
        s_ref[rs, :] = sj
        gt = sj > m
        m = jnp.where(gt, sj, m)
        g = jnp.where(gt, float(j), g)
    key = g * float(ROW_TILE) + rows_ref[0:ROW_TILE, :]
    top = jnp.max(m, axis=0, keepdims=True)
    idx = jnp.min(jnp.where(m == top, key, float(rows)), axis=0, keepdims=True)
    return top, idx


def _route_kernel(half_experts, x_ref, wq_ref, sk_ref, m_ref, hi_ref, gate_ref, v0, i0, v1, i1, bs, be, cs, ce,
                  sa, sb, key_rows):
    tt = x_ref.shape[0]
    q = jnp.dot(x_ref[...].astype(BF16), wq_ref[...], preferred_element_type=F32)
    qb = q.astype(BF16)
    sa[...] = _nt_dot(sk_ref[0, 0], qb[:, :LANES])
    sb[...] = _nt_dot(sk_ref[0, 1], qb[:, LANES:])
    key_rows[...] = lax.broadcasted_iota(jnp.int32, (N_KEYS, tt), 0).astype(F32)

    def first(r, prev):
        ma, ia = _scan_max(sa, key_rows, prev[0])
        mb, ib = _scan_max(sb, key_rows, prev[1])
        v0[pl.ds(r, 1), :] = ma
        i0[pl.ds(r, 1), :] = ia
        v1[pl.ds(r, 1), :] = mb
        i1[pl.ds(r, 1), :] = ib
        return ia, ib

    none = jnp.full((1, tt), -1.0, F32)
    lax.fori_loop(0, PEER_TOPK, first, (none, none))

    off = 0
    for a in range(PEER_TOPK):
        nb = PEER_TOPK // (a + 1)
        cs[off:off + nb, :] = v0[a:a + 1, :] + v1[0:nb, :]
        ce[off:off + nb, :] = i0[a:a + 1, :] * float(N_KEYS) + i1[0:nb, :]
        off += nb
    cs[off:, :] = jnp.full((cs.shape[0] - off, tt), -jnp.inf, F32)
    ce[off:, :] = jnp.full((cs.shape[0] - off, tt), -1.0, F32)
    pairs = PEER_TOPK // 2
    cand_rows = cs.shape[0]

    def second(r, prev):
        m, idx = _scan_max(cs, key_rows, prev)
        row = lax.shift_right_logical(r, 1) + pairs * (r & 1)
        bs[pl.ds(row, 1), :] = m
        picked = jnp.where(key_rows[0:cand_rows, :] == idx, ce[...], -1.0)
        be[pl.ds(row, 1), :] = jnp.max(picked, axis=0, keepdims=True)
        return idx

    lax.fori_loop(0, PEER_TOPK, second, none)

    best = bs[...]
    p = jnp.exp(best - jnp.max(best, axis=0, keepdims=True))
    gate = p / jnp.sum(p, axis=0, keepdims=True)
    e = be[...].astype(jnp.int32)
    high = (e >= half_experts).astype(jnp.int32)
    rows = (e - high * half_experts) * ROW_TILE
    words = rows[:pairs] | (rows[pairs:] << 16)
    tb = m_ref.shape[2]
    for k in range(tt // tb):
        lanes = slice(k * tb, (k + 1) * tb)
        gate_ref[k] = gate[:, lanes]
        hi_ref[k] = high[:, lanes]
        m_ref[k] = words[:, lanes]


def _route(x, wq, sk, tt, tb):
    n, d = x.shape
    half_experts = N_KEYS * N_KEYS // 2
    cand_rows = -(-sum(PEER_TOPK // (a + 1) for a in range(PEER_TOPK)) // ROW_TILE) * ROW_TILE
    out = jax.ShapeDtypeStruct((n // tb, SLOTS, tb), jnp.int32)
    ospec = pl.BlockSpec((tt // tb, PEER_TOPK, tb), lambda i, h: (i, h, 0))
    return pl.pallas_call(
        functools.partial(_route_kernel, half_experts),
        grid=(n // tt, PEER_HEADS),
        in_specs=[
            pl.BlockSpec((tt, d), lambda i, h: (i, 0)),
            pl.BlockSpec((d, 2 * LANES), lambda i, h: (0, h)),
            pl.BlockSpec((1, 2, N_KEYS, LANES), lambda i, h: (h, 0, 0, 0)),
        ],
        out_specs=[pl.BlockSpec((tt // tb, PEER_TOPK // 2, tb), lambda i, h: (i, h, 0)), ospec, ospec],
        out_shape=[jax.ShapeDtypeStruct((n // tb, SLOTS // 2, tb), jnp.int32), out,
                   jax.ShapeDtypeStruct(out.shape, F32)],
        scratch_shapes=([pltpu.VMEM((PEER_TOPK, tt), F32)] * 6 + [pltpu.VMEM((cand_rows, tt), F32)] * 2
                        + [pltpu.VMEM((N_KEYS, tt), F32)] * 3),
        compiler_params=_cparams(("arbitrary", "arbitrary")),
        name="peer_route",
    )(x, wq, sk)


def _bf16_bits(x):
    return pltpu.bitcast(x.astype(BF16).astype(F32), jnp.uint32)


def _pack_kernel(lo_ref, hi_ref, o_ref):
    half_rows = ROW_TILE // 2
    for i in range(lo_ref.shape[0] // ROW_TILE):
        src = slice(i * ROW_TILE, (i + 1) * ROW_TILE)
        for part, ref in enumerate((lo_ref, hi_ref)):
            for q in range(half_rows):
                a = _bf16_bits(ref[src, 2 * q * LANES:(2 * q + 1) * LANES])
                b = _bf16_bits(ref[src, (2 * q + 1) * LANES:(2 * q + 2) * LANES])
                word = b | lax.shift_right_logical(a, jnp.uint32(16))
                rows = pl.ds(i * ROW_TILE * ROW_TILE + part * half_rows + q, ROW_TILE, stride=ROW_TILE)
                o_ref[rows, :] = pltpu.bitcast(word, jnp.int32)


def _pack_table(tab, rows=PACK_TILE):
    n, d = tab.shape
    half = n // 2
    steps = half // rows
    return pl.pallas_call(
        _pack_kernel,
        grid=(steps,),
        in_specs=[pl.BlockSpec((rows, d), lambda i: (i, 0)),
                  pl.BlockSpec((rows, d), lambda i: (i + steps, 0))],
        out_specs=pl.BlockSpec((rows * ROW_TILE, LANES), lambda i: (i, 0)),
        out_shape=jax.ShapeDtypeStruct((half * ROW_TILE, LANES), jnp.int32),
        compiler_params=_cparams(("arbitrary",)),
        name="pack_table",
    )(tab, tab)


def _load_tile(tab_ref, row):
    return pltpu.bitcast(tab_ref[pl.ds(pl.multiple_of(row, ROW_TILE), ROW_TILE), :], BF16)


def _load_chunk(tab_ref, m_ref, c, t, tb):
    words = [m_ref[(c * (CHUNK // 2) + j) * tb + t] for j in range(CHUNK // 2)]
    return ([_load_tile(tab_ref, w & 0xFFFF) for w in words]
            + [_load_tile(tab_ref, lax.shift_right_logical(w, 16)) for w in words])


def _diag_mask(rows):
    r = lax.broadcasted_iota(jnp.int32, (rows, CHUNK * PACK_ROWS), 0)
    c = lax.broadcasted_iota(jnp.int32, (rows, CHUNK * PACK_ROWS), 1)
    return (r % PACK_ROWS) == (c % PACK_ROWS)


def _peer_act_kernel(m_ref, x_ref, hi_ref, gate_ref, sel_ref, tab_ref, coef_ref, r_scr):
    tb = x_ref.shape[0]
    diag = _diag_mask(2 * PACK_ROWS)

    def tokens(it, carry):
        for u in range(TOKEN_UNROLL):
            t = it * TOKEN_UNROLL + u
            xh, xl = _split_bf16(x_ref[t])
            lhs = jnp.concatenate([xh, xh, xl, xl] * ACT_LHS_COPIES, axis=0)
            for c in range(SLOTS // CHUNK):
                tiles = _load_chunk(tab_ref, m_ref, c, t, tb)
                g = jnp.concatenate(tiles, axis=0)
                o = jnp.where(diag, _nt_dot(lhs, g)[:2 * PACK_ROWS], 0.0)
                r_scr[pl.ds(t, 1), c * 256:(c + 1) * 256] = jnp.sum(o, axis=0, keepdims=True)
        return carry

    lax.fori_loop(0, tb // TOKEN_UNROLL, tokens, 0)
    act2 = _dot_select(r_scr[...], sel_ref[...])
    high = hi_ref[0].astype(F32).T > 0
    act = jnp.where(high, act2[:, SLOTS:], act2[:, :SLOTS])
    gelu = 0.5 * act * (1.0 + lax.erf(act * (1.0 / math.sqrt(2.0))))
    coef_ref[...] = gate_ref[0].T * gelu


def _peer_out_kernel(alpha, m_ref, x_ref, hi_ref, coef_ref, exp_ref, g_ref, b_ref, tab_ref, y_ref, ce_scr, h_scr):
    tb = x_ref.shape[0]
    diag = _diag_mask(PACK_ROWS)
    coef = coef_ref[...]
    high = hi_ref[0].astype(F32).T > 0
    c2 = jnp.concatenate([jnp.where(high, 0.0, coef), jnp.where(high, coef, 0.0)], axis=1)
    ce_scr[...] = _dot_select(c2, exp_ref[...])
    def tokens(it, carry):
        for u in range(TOKEN_UNROLL):
            t = it * TOKEN_UNROLL + u
            acc = jnp.zeros((PACK_ROWS, LANES), F32)
            for c in range(SLOTS // CHUNK):
                tiles = _load_chunk(tab_ref, m_ref, c, t, tb)
                g = jnp.concatenate(tiles, axis=0)
                ce = ce_scr[pl.ds(t, 1), c * 256:(c + 1) * 256]
                cm = jnp.where(diag, jnp.broadcast_to(ce, (PACK_ROWS, 256)), 0.0)
                ch, cl = _split_bf16(cm)
                lhs = jnp.concatenate([ch, cl] * OUT_LHS_COPIES, axis=0)
                o = jnp.dot(lhs, g, preferred_element_type=F32)[:2 * PACK_ROWS]
                acc = acc + o[:PACK_ROWS] + o[PACK_ROWS:]
            rows = pl.ds(pl.multiple_of(t * ROW_TILE, ROW_TILE), ROW_TILE)
            h_scr[rows, :] = alpha * x_ref[t] + acc[:ROW_TILE] + acc[ROW_TILE:]
        return carry

    lax.fori_loop(0, tb // TOKEN_UNROLL, tokens, 0)

    h = jnp.concatenate([h_scr[pl.ds(j, tb, stride=ROW_TILE), :] for j in range(ROW_TILE)], axis=1)
    y_ref[...] = _layer_norm_rows(h, g_ref[...], b_ref[...])


def _exp_matrix():
    return _sel_matrix().T


def _peer_out(m_flat, x3, hi, coef, tab, gain, bias, alpha, tb):
    n = x3.shape[0]
    d = ROW_TILE * LANES
    expand = _exp_matrix()
    return pl.pallas_call(
        functools.partial(_peer_out_kernel, alpha),
        grid=(n // tb,),
        in_specs=[
            pl.BlockSpec((tb * SLOTS // 2,), lambda i: (i,), memory_space=pltpu.SMEM),
            pl.BlockSpec((tb, ROW_TILE, LANES), lambda i: (i, 0, 0)),
            pl.BlockSpec((1, SLOTS, tb), lambda i: (i, 0, 0)),
            pl.BlockSpec((tb, SLOTS), lambda i: (i, 0)),
            pl.BlockSpec(expand.shape, lambda i: (0, 0)),
            pl.BlockSpec((1, d), lambda i: (0, 0)),
            pl.BlockSpec((1, d), lambda i: (0, 0)),
            pl.BlockSpec(tab.shape, lambda i: (0, 0), pipeline_mode=pl.Buffered(1)),
        ],
        out_specs=pl.BlockSpec((tb, d), lambda i: (i, 0)),
        out_shape=jax.ShapeDtypeStruct((n, d), F32),
        scratch_shapes=[pltpu.VMEM((tb, SLOTS * PACK_ROWS), F32), pltpu.VMEM((tb * ROW_TILE, LANES), F32)],
        compiler_params=_cparams(("arbitrary",)),
        name="peer_out",
    )(m_flat, x3, hi, coef, expand, gain.reshape(1, d), bias.reshape(1, d), tab)


def _sel_matrix():
    k = jnp.arange(SLOTS * PACK_ROWS)
    slot, row = k // PACK_ROWS, k % PACK_ROWS
    col = jnp.where(row < ROW_TILE, slot, SLOTS + slot)
    return (col[:, None] == jnp.arange(2 * SLOTS)[None, :]).astype(BF16)


def _peer_act(m_flat, x3, hi, gate, tab, tb):
    n = x3.shape[0]
    sel = _sel_matrix()
    return pl.pallas_call(
        _peer_act_kernel,
        grid=(n // tb,),
        in_specs=[
            pl.BlockSpec((tb * SLOTS // 2,), lambda i: (i,), memory_space=pltpu.SMEM),
            pl.BlockSpec((tb, ROW_TILE, LANES), lambda i: (i, 0, 0)),
            pl.BlockSpec((1, SLOTS, tb), lambda i: (i, 0, 0)),
            pl.BlockSpec((1, SLOTS, tb), lambda i: (i, 0, 0)),
            pl.BlockSpec(sel.shape, lambda i: (0, 0)),
            pl.BlockSpec(tab.shape, lambda i: (0, 0), pipeline_mode=pl.Buffered(1)),
        ],
        out_specs=pl.BlockSpec((tb, SLOTS), lambda i: (i, 0)),
        out_shape=jax.ShapeDtypeStruct((n, SLOTS), F32),
        scratch_shapes=[pltpu.VMEM((tb, SLOTS * PACK_ROWS), F32)],
        compiler_params=_cparams(("arbitrary",)),
        name="peer_act",
    )(m_flat, x3, hi, gate, sel, tab)


def _peer_layer(x1, wq, sk, u_tab, v_tab, gain, bias, alpha):
    n = x1.shape[0]
    m, hi, gate = _route(x1, wq, sk, min(ROUTE_TILE, n), PEER_TILE)
    m_flat = m.reshape(n * SLOTS // 2)
    x3 = x1.reshape(n, ROW_TILE, LANES)
    coef = _peer_act(m_flat, x3, hi, gate, u_tab, PEER_TILE)
    return _peer_out(m_flat, x3, hi, coef, v_tab, gain, bias, alpha, PEER_TILE)


ROUTE_TILE = 1024
PEER_TILE = 256
PROJ_TILE = 256
MIX_TILE = 256


def kernel(x_prompt, x_sample, cache_k_win, cache_v_win, state_conv, w_in, w_out, conv_w, conv_b, conv_ln_g,
           conv_ln_b, ln1_g, ln1_b, w_query, sub_keys, expert_u, expert_v, ln2_g, ln2_b):
    depth, d_model, in_cols = w_in.shape
    batch, seq, _ = x_prompt.shape
    dec_batch, dec_seq, _ = x_sample.shape
    conv_ch = conv_w.shape[2]
    att_w = (in_cols - 2 * conv_ch) // 3
    heads = att_w // HEAD_DIM
    lb = cache_k_win.shape[2]
    windows = tuple((WIN_STEPS * d, d) for d in DILATIONS)
    w_max = windows[-1][0]
    assert lb == w_max and seq >= w_max and d_model == ROW_TILE * LANES
    alpha = (2.0 * depth) ** 0.25

    hp = x_prompt.reshape(batch * seq, d_model)
    hs = x_sample.reshape(dec_batch * dec_seq, d_model)
    outs = [[] for _ in range(6)]
    for l in range(depth):
        w_in_l, w_out_l = w_in[l].astype(BF16), w_out[l].astype(BF16)
        wq_l, sk_l = w_query[l].astype(BF16), sub_keys[l].astype(BF16)
        u_tab, v_tab = _pack_table(expert_u[l]), _pack_table(expert_v[l])
        mix_args = (w_out_l, conv_w[l], conv_b[l], conv_ln_g[l], conv_ln_b[l], ln1_g[l], ln1_b[l], alpha)
        peer_args = (wq_l, sk_l, u_tab, v_tab, ln2_g[l], ln2_b[l], alpha)

        q, k, v, glu, kt, vt = _in_proj(hp, w_in_l, att_w, PROJ_TILE, seq, w_max)
        att = _attn_prompt(q, k, v, batch, seq)
        past = jnp.zeros((batch, CONV_K - 1, conv_ch), F32)
        x1 = _mix(hp, att, glu, past, *mix_args, batch, seq, MIX_TILE)
        outs[0].append(kt.reshape(batch, heads, HEAD_DIM, w_max).transpose(0, 3, 1, 2))
        outs[1].append(vt.reshape(batch, heads, HEAD_DIM, w_max).transpose(0, 3, 1, 2))
        outs[2].append(glu.reshape(batch, seq, conv_ch)[:, seq - (CONV_K - 1):])
        hp = _peer_layer(x1, *peer_args)

        q, k, v, glu = _in_proj(hs, w_in_l, att_w, dec_batch * dec_seq)
        new3 = lambda a: a.reshape(dec_batch, dec_seq, att_w)
        to_feature_major = lambda c: c.transpose(0, 2, 3, 1).reshape(dec_batch, att_w, lb)
        att, nk, nv = _attn_sample(new3(q), new3(k), new3(v), to_feature_major(cache_k_win[l]),
                                   to_feature_major(cache_v_win[l]), windows)
        x1 = _mix(hs, att.reshape(dec_batch * dec_seq, att_w), glu, state_conv[l], *mix_args,
                  dec_batch, dec_seq, dec_seq)
        conv_in = jnp.concatenate([state_conv[l], glu.reshape(dec_batch, dec_seq, conv_ch)], axis=1)
        to_position_major = lambda c: c.reshape(dec_batch, heads, HEAD_DIM, lb).transpose(0, 3, 1, 2)
        outs[3].append(to_position_major(nk))
        outs[4].append(to_position_major(nv))
        outs[5].append(conv_in[:, dec_seq:])
        hs = _peer_layer(x1, *peer_args)

    stack = lambda rows: jnp.stack(rows, 0)
    return (hp.reshape(batch, seq, d_model), hs.reshape(dec_batch, dec_seq, d_model),
            stack(outs[0]), stack(outs[1]), stack(outs[2]), stack(outs[3]), stack(outs[4]), stack(outs[5]))
```
